```python
import jax
import jax.numpy as jnp
from jax import lax
import numpy as np

D_MODEL = 2048
BATCH = 4
SEQ = 4096
DEPTH = 4

N_MIXERS = 4
DEEPNORM_ALPHA = (2.0 * DEPTH) ** 0.25
DEEPNORM_BETA = (8.0 * DEPTH) ** -0.25
LN_EPS = 1e-5
D_FF = 4 * D_MODEL
NEG_INF = -1e30

LRU_WIDTH = D_MODEL
LRU_BLOCKS = 8
LRU_BLOCK_W = LRU_WIDTH // LRU_BLOCKS
CONV_WIDTH = 4
LRU_C = 8.0

ATTN_HEAD_DIM = 128
ATTN_HEADS = D_MODEL // ATTN_HEAD_DIM
MOBA_BLOCK = 256
MOBA_TOPK = 3
MOBA_Q_CHUNK = 16
ROPE_THETA = 500000.0
ROPE_DIM = ATTN_HEAD_DIM // 4

RWKV_HEAD_DIM = 64
RWKV_HEADS = D_MODEL // RWKV_HEAD_DIM
RWKV_DECAY_LORA = 96
RWKV_AAA_LORA = 96
RWKV_GATE_LORA = 256
RWKV_LNX_EPS = 64e-5

POOL_WINDOWS = (2, 4, 8, 16)
POOL_GROUPS = len(POOL_WINDOWS)
POOL_GROUP_W = D_MODEL // POOL_GROUPS

kernel_name = "hybrid_rglru_moba_rwkv7_pool_deepnorm"


def _layers_of(mixer):
    return len(range(mixer, DEPTH, N_MIXERS))


def _layer_norm(x, g, b):
    x32 = x.astype(jnp.float32)
    mean = jnp.mean(x32, axis=-1, keepdims=True)
    var = jnp.mean(jnp.square(x32 - mean), axis=-1, keepdims=True)
    return ((x32 - mean) * lax.rsqrt(var + LN_EPS) * g + b).astype(x.dtype)


def _sqrelu_mlp(x, w1, w2):
    h = jax.nn.relu(x @ w1)
    return (h * h) @ w2


def _linear_recurrence_combine(left, right):
    a_l, b_l = left
    a_r, b_r = right
    return a_l * a_r, a_r * b_l + b_r


def _rglru_mixer(x, w_in, conv_w, conv_b, gate_a_w, gate_a_b, gate_x_w, gate_x_b, lam, w_out):
    bsz, seqlen, _ = x.shape
    gate, u = jnp.split(x @ w_in, 2, axis=-1)
    u = lax.conv_general_dilated(
        u, conv_w[:, None, :], window_strides=(1,), padding=[(CONV_WIDTH - 1, 0)],
        dimension_numbers=("NWC", "WIO", "NWC"), feature_group_count=LRU_WIDTH) + conv_b
    ub = u.reshape(bsz, seqlen, LRU_BLOCKS, LRU_BLOCK_W)
    r = jax.nn.sigmoid((jnp.einsum("bsnw,nwv->bsnv", ub, gate_a_w) + gate_a_b).astype(jnp.float32))
    i = jax.nn.sigmoid((jnp.einsum("bsnw,nwv->bsnv", ub, gate_x_w) + gate_x_b).astype(jnp.float32))
    log_a = -LRU_C * r * jax.nn.softplus(-lam.astype(jnp.float32)).reshape(LRU_BLOCKS, LRU_BLOCK_W)
    a = jnp.exp(log_a)
    b = ub.astype(jnp.float32) * i * jnp.sqrt(-jnp.expm1(2.0 * log_a))
    _, h = lax.associative_scan(_linear_recurrence_combine, (a, b), axis=1)
    h = h.reshape(bsz, seqlen, LRU_WIDTH).astype(x.dtype)
    return (jax.nn.gelu(gate) * h) @ w_out


def _partial_rope(t, cos, sin):
    half = ROPE_DIM // 2
    t32 = t.astype(jnp.float32)
    t1, t2 = t32[..., :half], t32[..., half:ROPE_DIM]
    c, s = cos[None, :, None, :], sin[None, :, None, :]
    return jnp.concatenate([t1 * c - t2 * s, t2 * c + t1 * s, t32[..., ROPE_DIM:]], axis=-1).astype(t.dtype)


def _moba_mixer(x, w_qkv, w_out):
    bsz, seqlen, _ = x.shape
    n_blk = -(-seqlen // MOBA_BLOCK)
    pad = n_blk * MOBA_BLOCK - seqlen
    top_k = min(MOBA_TOPK, n_blk)
    qkv = (x @ w_qkv).reshape(bsz, seqlen, 3, ATTN_HEADS, ATTN_HEAD_DIM)
    q, k, v = qkv[:, :, 0], qkv[:, :, 1], qkv[:, :, 2]
    pos = jnp.arange(seqlen, dtype=jnp.float32)
    inv_freq = ROPE_THETA ** (-jnp.arange(0, ROPE_DIM, 2, dtype=jnp.float32) / ROPE_DIM)
    ang = pos[:, None] * inv_freq[None, :]
    cos, sin = jnp.cos(ang), jnp.sin(ang)
    q = (_partial_rope(q, cos, sin) * (ATTN_HEAD_DIM ** -0.5)).transpose(0, 2, 1, 3)
    k = _partial_rope(k, cos, sin).transpose(0, 2, 1, 3)
    v = v.transpose(0, 2, 1, 3)
    blk_shape = (bsz, ATTN_HEADS, n_blk, MOBA_BLOCK, ATTN_HEAD_DIM)
    k_blk = jnp.pad(k, ((0, 0), (0, 0), (0, pad), (0, 0))).reshape(blk_shape)
    v_blk = jnp.pad(v, ((0, 0), (0, 0), (0, pad), (0, 0))).reshape(blk_shape)
    k_mean = jnp.mean(k_blk.astype(jnp.float32), axis=3)
    gate = jnp.einsum("bhsd,bhnd->bhsn", q.astype(jnp.float32), k_mean)
    q_blk = jnp.arange(seqlen) // MOBA_BLOCK
    past = jnp.arange(n_blk)[None, :] < q_blk[:, None]
    gate = jnp.where(past, gate, NEG_INF)
    _, sel = lax.top_k(gate, top_k)
    sel_valid = sel < q_blk[:, None]
    b_idx = jnp.arange(bsz)[:, None, None, None]
    h_idx = jnp.arange(ATTN_HEADS)[None, :, None, None]
    key_off = jnp.arange(MOBA_BLOCK)
    n_sel = top_k * MOBA_BLOCK

    def chunk(c):
        start = c * MOBA_Q_CHUNK
        q_c = lax.dynamic_slice_in_dim(q, start, MOBA_Q_CHUNK, axis=2)
        sel_c = lax.dynamic_slice_in_dim(sel, start, MOBA_Q_CHUNK, axis=2)
        valid_c = lax.dynamic_slice_in_dim(sel_valid, start, MOBA_Q_CHUNK, axis=2)
        own = start // MOBA_BLOCK
        k_own = lax.dynamic_index_in_dim(k_blk, own, axis=2, keepdims=False)
        v_own = lax.dynamic_index_in_dim(v_blk, own, axis=2, keepdims=False)
        k_sel = k_blk[b_idx, h_idx, sel_c]
        v_sel = v_blk[b_idx, h_idx, sel_c]
        s_sel = jnp.einsum("bhcd,bhcjpd->bhcjp", q_c, k_sel, preferred_element_type=jnp.float32)
        s_sel = jnp.where(valid_c[..., None], s_sel, NEG_INF)
        s_own = jnp.einsum("bhcd,bhpd->bhcp", q_c, k_own, preferred_element_type=jnp.float32)
        q_pos = start + jnp.arange(MOBA_Q_CHUNK)
        k_pos = own * MOBA_BLOCK + key_off
        s_own = jnp.where(k_pos[None, :] <= q_pos[:, None], s_own, NEG_INF)
        s = jnp.concatenate([s_sel.reshape(bsz, ATTN_HEADS, MOBA_Q_CHUNK, n_sel), s_own], axis=-1)
        p = jax.nn.softmax(s, axis=-1).astype(v.dtype)
        p_sel = p[..., :n_sel].reshape(bsz, ATTN_HEADS, MOBA_Q_CHUNK, top_k, MOBA_BLOCK)
        p_own = p[..., n_sel:]
        return (jnp.einsum("bhcjp,bhcjpd->bhcd", p_sel, v_sel)
                + jnp.einsum("bhcp,bhpd->bhcd", p_own, v_own))

    o = lax.map(chunk, jnp.arange(seqlen // MOBA_Q_CHUNK))
    o = o.transpose(1, 0, 3, 2, 4).reshape(bsz, seqlen, ATTN_HEADS * ATTN_HEAD_DIM)
    return o @ w_out


def _token_shift(x):
    return jnp.pad(x, ((0, 0), (1, 0), (0, 0)))[:, :-1]


def _rwkv7_step(state, inp):
    r_t, w_t, k_t, v_t, kk_t, b_t = inp
    sa = jnp.einsum("bhvk,bhk->bhv", state, -kk_t)
    state = (state * w_t[:, :, None, :]
             + jnp.einsum("bhv,bhk->bhvk", sa, b_t)
             + jnp.einsum("bhv,bhk->bhvk", v_t, k_t))
    return state, jnp.einsum("bhvk,bhk->bhv", state, r_t)


def _rwkv7_mixer(x, mu, w_rkv, w0, w_w1, w_w2, a0, a_w1, a_w2, g_w1, g_w2,
                 k_k, k_a, r_k, lnx_g, lnx_b, w_out):
    f32 = jnp.float32
    bsz, seqlen, _ = x.shape
    hs = (bsz, seqlen, RWKV_HEADS, RWKV_HEAD_DIM)
    xx = _token_shift(x) - x
    x_rkv = x[:, :, None, :] + xx[:, :, None, :] * mu[:3]
    rkv = jnp.einsum("bsgd,gde->bsge", x_rkv, w_rkv)
    r, k, v = rkv[:, :, 0], rkv[:, :, 1], rkv[:, :, 2]
    xw = x + xx * mu[3]
    xa = x + xx * mu[4]
    xg = x + xx * mu[5]
    w_log = -jax.nn.softplus(-(w0 + jnp.tanh(xw @ w_w1) @ w_w2).astype(f32)) - 0.5
    decay = jnp.exp(-jnp.exp(w_log)).reshape(hs)
    a = jax.nn.sigmoid((a0 + (xa @ a_w1) @ a_w2).astype(f32))
    g = jax.nn.sigmoid(xg @ g_w1) @ g_w2
    kk = (k * k_k).astype(f32).reshape(hs)
    kk = kk / jnp.maximum(jnp.sqrt(jnp.sum(kk * kk, axis=-1, keepdims=True)), 1e-12)
    k32 = (k.astype(f32) * (1.0 + (a - 1.0) * k_a)).reshape(hs)
    r32 = r.astype(f32).reshape(hs)
    v32 = v.astype(f32).reshape(hs)
    b = kk * a.reshape(hs)
    xs = tuple(jnp.moveaxis(t, 1, 0) for t in (r32, decay, k32, v32, kk, b))
    state0 = jnp.zeros((bsz, RWKV_HEADS, RWKV_HEAD_DIM, RWKV_HEAD_DIM), f32)
    _, y = lax.scan(_rwkv7_step, state0, xs)
    y = jnp.moveaxis(y, 0, 1)
    mean = jnp.mean(y, axis=-1, keepdims=True)
    var = jnp.mean(jnp.square(y - mean), axis=-1, keepdims=True)
    y = ((y - mean) * lax.rsqrt(var + RWKV_LNX_EPS)).reshape(bsz, seqlen, D_MODEL) * lnx_g + lnx_b
    bonus = jnp.sum(r32 * k32 * r_k, axis=-1, keepdims=True) * v32
    y = y + bonus.reshape(bsz, seqlen, D_MODEL)
    return (y.astype(x.dtype) * g) @ w_out


def _pool_mixer(x, w_pool, pool_scale):
    bsz, seqlen, _ = x.shape
    x32 = x.astype(jnp.float32)
    csum = jnp.pad(jnp.cumsum(x32, axis=1), ((0, 0), (1, 0), (0, 0)))
    t = jnp.arange(seqlen)
    groups = []
    for gi, w in enumerate(POOL_WINDOWS):
        sl = slice(gi * POOL_GROUP_W, (gi + 1) * POOL_GROUP_W)
        upper = csum[:, 1:, sl]
        lower = jnp.pad(csum[:, :seqlen - w + 1, sl], ((0, 0), (w - 1, 0), (0, 0)))
        count = jnp.minimum(t + 1, w).astype(jnp.float32)[None, :, None]
        groups.append((upper - lower) / count - x32[:, :, sl])
    pooled = jnp.stack(groups, axis=2).astype(x.dtype)
    y = jnp.einsum("bsgc,gce->bsge", pooled, w_pool).reshape(bsz, seqlen, D_MODEL)
    return y * pool_scale


def setup_inputs(seed: int = 0) -> dict:
    key = jax.random.key(seed)
    keys = list(jax.random.split(key, 48))
    f32 = jnp.float32

    def nrm(shape, scale):
        return jax.random.normal(keys.pop(), shape, f32) * scale

    def uni(shape, lo, hi):
        return jax.random.uniform(keys.pop(), shape, f32, lo, hi)

    n_a, n_b, n_c, n_d = (_layers_of(m) for m in range(N_MIXERS))
    D, R = D_MODEL, LRU_WIDTH
    beta = DEEPNORM_BETA
    u = uni((n_a, R), 0.9, 0.999)
    a_base = u ** (1.0 / LRU_C)
    return {
        "x": nrm((BATCH, SEQ, D), 1.0),
        "ln_g": 1.0 + nrm((DEPTH, 2, D), 0.05),
        "ln_b": nrm((DEPTH, 2, D), 0.01),
        "mlp_w1": nrm((DEPTH, D, D_FF), D ** -0.5),
        "mlp_w2": nrm((DEPTH, D_FF, D), D_FF ** -0.5 * beta),
        "rg_w_in": nrm((n_a, D, 2 * R), D ** -0.5),
        "rg_conv_w": nrm((n_a, CONV_WIDTH, R), CONV_WIDTH ** -0.5),
        "rg_conv_b": nrm((n_a, R), 0.01),
        "rg_gate_a_w": nrm((n_a, LRU_BLOCKS, LRU_BLOCK_W, LRU_BLOCK_W), LRU_BLOCK_W ** -0.5),
        "rg_gate_a_b": nrm((n_a, LRU_BLOCKS, LRU_BLOCK_W), 0.01),
        "rg_gate_x_w": nrm((n_a, LRU_BLOCKS, LRU_BLOCK_W, LRU_BLOCK_W), LRU_BLOCK_W ** -0.5),
        "rg_gate_x_b": nrm((n_a, LRU_BLOCKS, LRU_BLOCK_W), 0.01),
        "rg_lambda": jnp.log(a_base) - jnp.log1p(-a_base),
        "rg_w_out": nrm((n_a, R, D), R ** -0.5 * beta),
        "moba_w_qkv": nrm((n_b, D, 3 * D), D ** -0.5),
        "moba_w_out": nrm((n_b, D, D), D ** -0.5 * beta),
        "rwkv_mu": uni((n_c, 6, D), 0.0, 1.0),
        "rwkv_w_rkv": nrm((n_c, 3, D, D), D ** -0.5),
        "rwkv_w0": uni((n_c, D), -5.0, -1.0),
        "rwkv_w_w1": nrm((n_c, D, RWKV_DECAY_LORA), D ** -0.5),
        "rwkv_w_w2": nrm((n_c, RWKV_DECAY_LORA, D), 0.5 * RWKV_DECAY_LORA ** -0.5),
        "rwkv_a0": nrm((n_c, D), 0.3),
        "rwkv_a_w1": nrm((n_c, D, RWKV_AAA_LORA), D ** -0.5),
        "rwkv_a_w2": nrm((n_c, RWKV_AAA_LORA, D), 0.5 * RWKV_AAA_LORA ** -0.5),
        "rwkv_g_w1": nrm((n_c, D, RWKV_GATE_LORA), D ** -0.5),
        "rwkv_g_w2": nrm((n_c, RWKV_GATE_LORA, D), RWKV_GATE_LORA ** -0.5),
        "rwkv_k_k": 0.85 + nrm((n_c, D), 0.05),
        "rwkv_k_a": 1.0 + nrm((n_c, D), 0.05),
        "rwkv_r_k": nrm((n_c, RWKV_HEADS, RWKV_HEAD_DIM), 0.1),
        "rwkv_lnx_g": 1.0 + nrm((n_c, D), 0.05),
        "rwkv_lnx_b": nrm((n_c, D), 0.01),
        "rwkv_w_out": nrm((n_c, D, D), D ** -0.5 * beta),
        "pool_w": nrm((n_d, POOL_GROUPS, POOL_GROUP_W, POOL_GROUP_W), POOL_GROUP_W ** -0.5 * beta),
        "pool_scale": 1.0 + nrm((n_d, D), 0.05),
    }


def reference(x, ln_g, ln_b, mlp_w1, mlp_w2,
              rg_w_in, rg_conv_w, rg_conv_b, rg_gate_a_w, rg_gate_a_b, rg_gate_x_w, rg_gate_x_b,
              rg_lambda, rg_w_out,
              moba_w_qkv, moba_w_out,
              rwkv_mu, rwkv_w_rkv, rwkv_w0, rwkv_w_w1, rwkv_w_w2, rwkv_a0, rwkv_a_w1, rwkv_a_w2,
              rwkv_g_w1, rwkv_g_w2, rwkv_k_k, rwkv_k_a, rwkv_r_k, rwkv_lnx_g, rwkv_lnx_b, rwkv_w_out,
              pool_w, pool_scale):
    counts = [0] * N_MIXERS
    for layer in range(DEPTH):
        m = layer % N_MIXERS
        j = counts[m]
        counts[m] += 1
        if m == 0:
            y = _rglru_mixer(x, rg_w_in[j], rg_conv_w[j], rg_conv_b[j], rg_gate_a_w[j], rg_gate_a_b[j],
                             rg_gate_x_w[j], rg_gate_x_b[j], rg_lambda[j], rg_w_out[j])
        elif m == 1:
            y = _moba_mixer(x, moba_w_qkv[j], moba_w_out[j])
        elif m == 2:
            y = _rwkv7_mixer(x, rwkv_mu[j], rwkv_w_rkv[j], rwkv_w0[j], rwkv_w_w1[j], rwkv_w_w2[j],
                             rwkv_a0[j], rwkv_a_w1[j], rwkv_a_w2[j], rwkv_g_w1[j], rwkv_g_w2[j],
                             rwkv_k_k[j], rwkv_k_a[j], rwkv_r_k[j], rwkv_lnx_g[j], rwkv_lnx_b[j],
                             rwkv_w_out[j])
        else:
            y = _pool_mixer(x, pool_w[j], pool_scale[j])
        x = _layer_norm(DEEPNORM_ALPHA * x + y, ln_g[layer, 0], ln_b[layer, 0])
        x = _layer_norm(DEEPNORM_ALPHA * x + _sqrelu_mlp(x, mlp_w1[layer], mlp_w2[layer]),
                        ln_g[layer, 1], ln_b[layer, 1])
    return x
```

```python
import functools

import jax
import jax.numpy as jnp
from jax import lax
from jax.experimental import pallas as pl
from jax.experimental.pallas import tpu as pltpu

F32 = jnp.float32
BF16 = jnp.bfloat16
HIGHEST = lax.Precision.HIGHEST

LN_EPS = 1e-5
NEG_INF = -1e30
VMEM_LIMIT_BYTES = 48 * 1024 * 1024

LRU_C = 8.0
CONV_WIDTH = 4
ATTN_HEAD_DIM = 128
MOBA_BLOCK = 256
MOBA_TOPK = 3
ROPE_THETA = 500000.0
ROPE_DIM = ATTN_HEAD_DIM // 4
RWKV_HEAD_DIM = 64
RWKV_PAIR = 2 * RWKV_HEAD_DIM
RWKV_CHUNK = 64
RWKV_LNX_EPS = 64e-5
POOL_WINDOWS = (2, 4, 8, 16)
POOL_HALO = 16
LANES = 128
SUBLANES = 8


def _params(n_axes):
    return pltpu.CompilerParams(dimension_semantics=("arbitrary",) * n_axes, vmem_limit_bytes=VMEM_LIMIT_BYTES)


def _layer_norm_rows(z, g, b):
    mean = jnp.mean(z, axis=-1, keepdims=True)
    zc = z - mean
    var = jnp.mean(zc * zc, axis=-1, keepdims=True)
    return zc * lax.rsqrt(var + LN_EPS) * g + b


def _softplus(z):
    return jnp.maximum(z, 0.0) + jnp.log1p(jnp.exp(-jnp.abs(z)))


def _dot(a, b, precision=None):
    return jnp.dot(a, b, preferred_element_type=F32, precision=precision)


def _dot_nt(a, b, precision=None):
    return lax.dot_general(a, b, (((1,), (1,)), ((), ())), preferred_element_type=F32, precision=precision)


def _mm_kernel(a_ref, w_ref, o_ref):
    o_ref[...] = _dot(a_ref[...].astype(BF16), w_ref[...]).astype(o_ref.dtype)


def _matmul(a, w, out_dtype, tm=512, tn=1024):
    n, k = a.shape
    nout = w.shape[1]
    tn = min(tn, nout)
    return pl.pallas_call(
        _mm_kernel,
        grid=(nout // tn, n // tm),
        in_specs=[pl.BlockSpec((tm, k), lambda j, i: (i, 0)), pl.BlockSpec((k, tn), lambda j, i: (0, j))],
        out_specs=pl.BlockSpec((tm, tn), lambda j, i: (i, j)),
        out_shape=jax.ShapeDtypeStruct((n, nout), out_dtype),
        compiler_params=_params(2),
        name="matmul",
    )(a, w)


def _mm_ln_kernel(a_ref, w_ref, x_ref, g_ref, b_ref, o_ref, *, alpha):
    y = _dot(a_ref[...].astype(BF16), w_ref[...])
    o_ref[...] = _layer_norm_rows(alpha * x_ref[...] + y, g_ref[...], b_ref[...])


def _matmul_ln(a, w, x, g, b, alpha, tm=512):
    n, k = a.shape
    d = w.shape[1]
    return pl.pallas_call(
        functools.partial(_mm_ln_kernel, alpha=alpha),
        grid=(n // tm,),
        in_specs=[
            pl.BlockSpec((tm, k), lambda i: (i, 0)),
            pl.BlockSpec((k, d), lambda i: (0, 0)),
            pl.BlockSpec((tm, d), lambda i: (i, 0)),
            pl.BlockSpec((1, d), lambda i: (0, 0)),
            pl.BlockSpec((1, d), lambda i: (0, 0)),
        ],
        out_specs=pl.BlockSpec((tm, d), lambda i: (i, 0)),
        out_shape=jax.ShapeDtypeStruct((n, d), F32),
        compiler_params=_params(1),
        name="matmul_ln",
    )(a, w, x, g, b)


def _mlp_kernel(x_ref, w1_ref, w2_ref, g_ref, b_ref, o_ref, xb_ref, acc_ref, *, alpha):
    f = pl.program_id(1)

    @pl.when(f == 0)
    def _():
        xb_ref[...] = x_ref[...].astype(BF16)
        acc_ref[...] = jnp.zeros_like(acc_ref)

    h = jnp.maximum(_dot(xb_ref[...], w1_ref[...]), 0.0)
    acc_ref[...] += _dot((h * h).astype(BF16), w2_ref[...])

    @pl.when(f == pl.num_programs(1) - 1)
    def _():
        o_ref[...] = _layer_norm_rows(alpha * x_ref[...] + acc_ref[...], g_ref[...], b_ref[...])


def _mlp_ln(x, w1, w2, g, b, alpha, tm=512, tf=512):
    n, d = x.shape
    dff = w1.shape[1]
    return pl.pallas_call(
        functools.partial(_mlp_kernel, alpha=alpha),
        grid=(n // tm, dff // tf),
        in_specs=[
            pl.BlockSpec((tm, d), lambda i, f: (i, 0)),
            pl.BlockSpec((d, tf), lambda i, f: (0, f)),
            pl.BlockSpec((tf, d), lambda i, f: (f, 0)),
            pl.BlockSpec((1, d), lambda i, f: (0, 0)),
            pl.BlockSpec((1, d), lambda i, f: (0, 0)),
        ],
        out_specs=pl.BlockSpec((tm, d), lambda i, f: (i, 0)),
        out_shape=jax.ShapeDtypeStruct((n, d), F32),
        scratch_shapes=[pltpu.VMEM((tm, d), BF16), pltpu.VMEM((tm, d), F32)],
        compiler_params=_params(2),
        name="mlp_ln",
    )(x, w1, w2, g, b)


def _rglru_kernel(gate_ref, u_ref, cw_ref, cb_ref, aw_ref, ab_ref, xw_ref, xb_ref, lam_ref, o_ref, hc_ref, ut_ref, *, ts):
    @pl.when(pl.program_id(2) == 0)
    def _():
        hc_ref[...] = jnp.zeros_like(hc_ref)
        ut_ref[...] = jnp.zeros_like(ut_ref)

    u = u_ref[...]
    ue = jnp.concatenate([ut_ref[...], u], axis=0)
    conv = cw_ref[CONV_WIDTH - 1 : CONV_WIDTH, :] * u + cb_ref[...]
    for d in range(1, CONV_WIDTH):
        conv = conv + cw_ref[CONV_WIDTH - 1 - d : CONV_WIDTH - d, :] * pltpu.roll(ue, d, axis=0)[SUBLANES:]
    ut_ref[...] = u[ts - SUBLANES :, :]

    cb16 = conv.astype(BF16)
    r = jax.nn.sigmoid(_dot(cb16, aw_ref[0]) + ab_ref[...])
    i = jax.nn.sigmoid(_dot(cb16, xw_ref[0]) + xb_ref[...])
    log_a = (-LRU_C) * r * _softplus(-lam_ref[...])
    a = jnp.exp(log_a)
    b = conv * i * jnp.sqrt(1.0 - a * a)

    row = lax.broadcasted_iota(jnp.int32, a.shape, 0)
    d = 1
    while d < ts:
        keep = row >= d
        a_sh = jnp.where(keep, pltpu.roll(a, d, axis=0), 1.0)
        b_sh = jnp.where(keep, pltpu.roll(b, d, axis=0), 0.0)
        b = a * b_sh + b
        a = a * a_sh
        d *= 2
    h = a * hc_ref[0:1, :] + b
    hc_ref[...] = jnp.broadcast_to(h[ts - 1 : ts, :], hc_ref.shape)
    o_ref[...] = (jax.nn.gelu(gate_ref[...]) * h).astype(o_ref.dtype)


def _rglru(proj, conv_w, conv_b, gate_a_w, gate_a_b, gate_x_w, gate_x_b, lam, bsz, seqlen, ts=512):
    n, r2 = proj.shape
    r = r2 // 2
    nblk, w, _ = gate_a_w.shape
    ts = min(ts, seqlen)
    nt = seqlen // ts
    row_spec = pl.BlockSpec((1, w), lambda b, j, t: (0, j))
    gw_spec = pl.BlockSpec((1, w, w), lambda b, j, t: (j, 0, 0))
    return pl.pallas_call(
        functools.partial(_rglru_kernel, ts=ts),
        grid=(bsz, nblk, nt),
        in_specs=[
            pl.BlockSpec((ts, w), lambda b, j, t: (b * nt + t, j)),
            pl.BlockSpec((ts, w), lambda b, j, t: (b * nt + t, nblk + j)),
            pl.BlockSpec((CONV_WIDTH, w), lambda b, j, t: (0, j)),
            row_spec,
            gw_spec,
            row_spec,
            gw_spec,
            row_spec,
            row_spec,
        ],
        out_specs=pl.BlockSpec((ts, w), lambda b, j, t: (b * nt + t, j)),
        out_shape=jax.ShapeDtypeStruct((n, r), BF16),
        scratch_shapes=[pltpu.VMEM((SUBLANES, w), F32), pltpu.VMEM((SUBLANES, w), F32)],
        compiler_params=_params(3),
        name="rglru",
    )(proj, proj, conv_w, conv_b, gate_a_w, gate_a_b, gate_x_w, gate_x_b, lam)


def _qkv_kernel(x_ref, w_ref, c_ref, s1_ref, s2_ref, o_ref, *, tn, d_model, scale):
    acc = _dot(x_ref[...].astype(BF16), w_ref[...])
    rep = tn // ATTN_HEAD_DIM
    c = jnp.concatenate([c_ref[...]] * rep, axis=1)
    s1 = jnp.concatenate([s1_ref[...]] * rep, axis=1)
    s2 = jnp.concatenate([s2_ref[...]] * rep, axis=1)
    half = ROPE_DIM // 2
    roped = acc * c + pltpu.roll(acc, half, axis=1) * s1 + pltpu.roll(acc, tn - half, axis=1) * s2
    col0 = pl.program_id(0) * tn
    out = jnp.where(col0 >= 2 * d_model, acc, roped) * jnp.where(col0 < d_model, scale, 1.0)
    o_ref[...] = out.astype(o_ref.dtype)


def _rope_tables(seqlen):
    half = ROPE_DIM // 2
    pos = jnp.arange(seqlen, dtype=F32)
    inv_freq = ROPE_THETA ** (-jnp.arange(0, ROPE_DIM, 2, dtype=F32) / ROPE_DIM)
    ang = pos[:, None] * inv_freq[None, :]
    cos, sin = jnp.cos(ang), jnp.sin(ang)
    pad = ATTN_HEAD_DIM - ROPE_DIM
    c = jnp.concatenate([cos, cos, jnp.ones((seqlen, pad), F32)], axis=1)
    s1 = jnp.concatenate([jnp.zeros((seqlen, half), F32), sin, jnp.zeros((seqlen, pad), F32)], axis=1)
    s2 = jnp.concatenate([-sin, jnp.zeros((seqlen, half + pad), F32)], axis=1)
    return c, s1, s2


def _qkv_rope(x, w_qkv, seqlen, tm=512, tn=512):
    n, d = x.shape
    nout = w_qkv.shape[1]
    tm = min(tm, seqlen)
    tn = min(tn, d)
    c, s1, s2 = _rope_tables(seqlen)
    nt = seqlen // tm
    tab_spec = pl.BlockSpec((tm, ATTN_HEAD_DIM), lambda j, i: (i % nt, 0))
    return pl.pallas_call(
        functools.partial(_qkv_kernel, tn=tn, d_model=d, scale=ATTN_HEAD_DIM**-0.5),
        grid=(nout // tn, n // tm),
        in_specs=[pl.BlockSpec((tm, d), lambda j, i: (i, 0)), pl.BlockSpec((d, tn), lambda j, i: (0, j)), tab_spec, tab_spec, tab_spec],
        out_specs=pl.BlockSpec((tm, tn), lambda j, i: (i, j)),
        out_shape=jax.ShapeDtypeStruct((n, nout), BF16),
        compiler_params=_params(2),
        name="qkv_rope",
    )(x, w_qkv, c, s1, s2)


def _moba_kernel(q_ref, k_ref, v_ref, o_ref, kmean_ref, vt_ref, sel_ref, *, nblk):
    i = pl.program_id(2)
    blk, hd = MOBA_BLOCK, ATTN_HEAD_DIM

    @pl.when(i == 0)
    def _():
        kf = k_ref[...].astype(F32).reshape(nblk, blk, hd)
        kmean_ref[...] = jnp.sum(kf, axis=1) * (1.0 / blk)
        vt_ref[...] = v_ref[...].astype(F32).T.astype(BF16)

    q = q_ref[...]

    gate = _dot_nt(kmean_ref[...], q.astype(F32), precision=HIGHEST)
    blk_id = lax.broadcasted_iota(jnp.int32, gate.shape, 0)
    past = jnp.where(blk_id < i, 1.0, 0.0)
    rows = []
    for n in range(nblk):
        gn = gate[n : n + 1, :]
        tie = jnp.where(blk_id < n, 1.0, 0.0)
        beats = jnp.where(gate > gn, 1.0, jnp.where(gate == gn, tie, 0.0)) * past
        cnt = jnp.sum(beats, axis=0, keepdims=True)
        rows.append(jnp.where(cnt < MOBA_TOPK, 1.0, 0.0) * jnp.where(n < i, 1.0, 0.0))
    sel_ref[...] = jnp.concatenate(rows, axis=0)

    off = pl.multiple_of(i * blk, blk)
    s = _dot_nt(k_ref[pl.ds(off, blk), :], q)
    key_pos = lax.broadcasted_iota(jnp.int32, s.shape, 0)
    qry_pos = lax.broadcasted_iota(jnp.int32, s.shape, 1)
    s = jnp.where(key_pos <= qry_pos, s, NEG_INF)
    m = jnp.max(s, axis=0, keepdims=True)
    p = jnp.exp(s - m)
    l = jnp.sum(p, axis=0, keepdims=True)
    acc = _dot(vt_ref[:, pl.ds(off, blk)], p.astype(BF16))

    def body(j, carry):
        m, l, acc = carry
        off = pl.multiple_of(j * blk, blk)
        s = _dot_nt(k_ref[pl.ds(off, blk), :], q)
        s = jnp.where(sel_ref[pl.ds(j, 1), :] > 0.5, s, NEG_INF)
        m_new = jnp.maximum(m, jnp.max(s, axis=0, keepdims=True))
        alpha = jnp.exp(m - m_new)
        p = jnp.exp(s - m_new)
        l = alpha * l + jnp.sum(p, axis=0, keepdims=True)
        acc = acc * alpha + _dot(vt_ref[:, pl.ds(off, blk)], p.astype(BF16))
        return m_new, l, acc

    m, l, acc = lax.fori_loop(0, i, body, (m, l, acc))
    o_ref[...] = (acc / l).T.astype(o_ref.dtype)


def _moba_attention(qkv, bsz, seqlen):
    n, d3 = qkv.shape
    d = d3 // 3
    nheads = d // ATTN_HEAD_DIM
    nblk = seqlen // MOBA_BLOCK
    blk, hd = MOBA_BLOCK, ATTN_HEAD_DIM
    return pl.pallas_call(
        functools.partial(_moba_kernel, nblk=nblk),
        grid=(bsz, nheads, nblk),
        in_specs=[
            pl.BlockSpec((blk, hd), lambda b, h, i: (b * nblk + i, h)),
            pl.BlockSpec((seqlen, hd), lambda b, h, i: (b, nheads + h)),
            pl.BlockSpec((seqlen, hd), lambda b, h, i: (b, 2 * nheads + h)),
        ],
        out_specs=pl.BlockSpec((blk, hd), lambda b, h, i: (b * nblk + i, h)),
        out_shape=jax.ShapeDtypeStruct((n, d), BF16),
        scratch_shapes=[pltpu.VMEM((nblk, hd), F32), pltpu.VMEM((hd, seqlen), BF16), pltpu.VMEM((nblk, blk), F32)],
        compiler_params=_params(3),
        name="moba_attention",
    )(qkv, qkv, qkv)


def _token_shift_delta(x, prev_ref, is_start):
    prev_row = jnp.where(is_start, 0.0, prev_ref[SUBLANES - 1 : SUBLANES, :])
    row = lax.broadcasted_iota(jnp.int32, x.shape, 0)
    return jnp.where(row == 0, prev_row, pltpu.roll(x, 1, axis=0)) - x


def _rwkv_proj_kernel(x_ref, prev_ref, mu_ref, w_ref, o_ref, *, tm, seqlen):
    x = x_ref[...]
    xx = _token_shift_delta(x, prev_ref, (pl.program_id(1) * tm) % seqlen == 0)
    o_ref[0] = _dot((x + xx * mu_ref[0]).astype(BF16), w_ref[0])


def _rwkv_proj(x, mu, w_rkv, seqlen, tm=512):
    n, d = x.shape
    tm = min(tm, seqlen)
    step = tm // SUBLANES
    return pl.pallas_call(
        functools.partial(_rwkv_proj_kernel, tm=tm, seqlen=seqlen),
        grid=(3, n // tm),
        in_specs=[
            pl.BlockSpec((tm, d), lambda g, i: (i, 0)),
            pl.BlockSpec((SUBLANES, d), lambda g, i: (jnp.maximum(i * step - 1, 0), 0)),
            pl.BlockSpec((1, 1, d), lambda g, i: (g, 0, 0)),
            pl.BlockSpec((1, d, d), lambda g, i: (g, 0, 0)),
        ],
        out_specs=pl.BlockSpec((1, tm, d), lambda g, i: (g, i, 0)),
        out_shape=jax.ShapeDtypeStruct((3, n, d), F32),
        compiler_params=_params(2),
        name="rwkv_proj",
    )(x, x, mu.reshape(mu.shape[0], 1, d), w_rkv)


def _rwkv_lora_kernel(x_ref, prev_ref, mu_ref, ww1_ref, ww2_ref, aw1_ref, aw2_ref, gw1_ref, gw2_ref, wo_ref, ao_ref, go_ref, *, tm, seqlen):
    x = x_ref[...]
    xx = _token_shift_delta(x, prev_ref, (pl.program_id(0) * tm) % seqlen == 0)
    xw = (x + xx * mu_ref[3:4, :]).astype(BF16)
    xa = (x + xx * mu_ref[4:5, :]).astype(BF16)
    xg = (x + xx * mu_ref[5:6, :]).astype(BF16)
    wo_ref[...] = _dot(jnp.tanh(_dot(xw, ww1_ref[...])).astype(BF16), ww2_ref[...])
    ao_ref[...] = _dot(_dot(xa, aw1_ref[...]).astype(BF16), aw2_ref[...])
    go_ref[...] = _dot(jax.nn.sigmoid(_dot(xg, gw1_ref[...])).astype(BF16), gw2_ref[...])


def _pad_lora(w1, w2):
    rank = w1.shape[1]
    pad = (-rank) % LANES
    return jnp.pad(w1, ((0, 0), (0, pad))), jnp.pad(w2, ((0, pad), (0, 0)))


def _rwkv_lora(x, mu, w_w1, w_w2, a_w1, a_w2, g_w1, g_w2, seqlen, tm=256):
    n, d = x.shape
    tm = min(tm, seqlen)
    step = tm // SUBLANES
    ws = [*_pad_lora(w_w1, w_w2), *_pad_lora(a_w1, a_w2), *_pad_lora(g_w1, g_w2)]
    full = lambda a: pl.BlockSpec(a.shape, lambda i: (0, 0))
    out_spec = pl.BlockSpec((tm, d), lambda i: (i, 0))
    return pl.pallas_call(
        functools.partial(_rwkv_lora_kernel, tm=tm, seqlen=seqlen),
        grid=(n // tm,),
        in_specs=[
            pl.BlockSpec((tm, d), lambda i: (i, 0)),
            pl.BlockSpec((SUBLANES, d), lambda i: (jnp.maximum(i * step - 1, 0), 0)),
            full(mu),
            *[full(w) for w in ws],
        ],
        out_specs=[out_spec, out_spec, out_spec],
        out_shape=[jax.ShapeDtypeStruct((n, d), F32)] * 3,
        compiler_params=_params(1),
        name="rwkv_lora",
    )(x, x, mu, *ws)


def _rwkv_core_kernel(r_ref, k_ref, v_ref, wpre_ref, apre_ref, g_ref, w0_ref, a0_ref, kk_ref, ka_ref, rk_ref, lg_ref, lb_ref,
                      o_ref, st_ref, *, tb):
    c = RWKV_CHUNK
    hd = RWKV_HEAD_DIM

    @pl.when(pl.program_id(2) == 0)
    def _():
        st_ref[...] = jnp.zeros_like(st_ref)

    lane = lax.broadcasted_iota(jnp.int32, (1, RWKV_PAIR), 1)
    m0 = jnp.where(lane < hd, 1.0, 0.0)
    m1 = 1.0 - m0
    ri = lax.broadcasted_iota(jnp.int32, (RWKV_PAIR, RWKV_PAIR), 0)
    ci = lax.broadcasted_iota(jnp.int32, (RWKV_PAIR, RWKV_PAIR), 1)
    same = (ri < hd) == (ci < hd)
    ones_bd = jnp.where(same, 1.0, 0.0)
    strict = jnp.where(same & (ci < ri), 1.0, 0.0)
    incl = jnp.where(same & (ci <= ri), 1.0, 0.0)
    eye = jnp.where(ri == ci, 1.0, 0.0)

    def head_sum(z):
        return _dot(z, ones_bd, precision=HIGHEST)

    def stack(z):
        return jnp.concatenate([z * m0, z * m1], axis=0)

    def hdot(a, b):
        return _dot(a, b, precision=HIGHEST)

    r = r_ref[0]
    k = k_ref[0]
    v = v_ref[0]
    w_log = -_softplus(-(w0_ref[...] + wpre_ref[...])) - 0.5
    lw = -jnp.exp(w_log)
    a = jax.nn.sigmoid(a0_ref[...] + apre_ref[...])
    kk = k * kk_ref[...]
    kk = kk / jnp.maximum(jnp.sqrt(head_sum(kk * kk)), 1e-12)
    k2 = k * (1.0 + (a - 1.0) * ka_ref[...])
    bv = kk * a

    rowc = lax.broadcasted_iota(jnp.int32, lw.shape, 0) % c
    cl = lw
    d = 1
    while d < c:
        cl = cl + jnp.where(rowc >= d, pltpu.roll(cl, d, axis=0), 0.0)
        d *= 2

    ys = []
    for ch in range(tb // c):
        sl = slice(ch * c, (ch + 1) * c)
        clc = cl[sl]
        cle = clc[c - 1 : c, :]
        e_pos = jnp.exp(clc)
        e_neg = jnp.exp(-clc)
        e_end = jnp.exp(cle - clc)
        rt_s = stack(r[sl] * e_pos)
        at_s = stack(-kk[sl] * jnp.exp(clc - lw[sl]))
        bt_s = stack(bv[sl] * e_neg)
        kt_s = stack(k2[sl] * e_neg)
        bend_s = stack(bv[sl] * e_end)
        kend_s = stack(k2[sl] * e_end)
        v_s = stack(v[sl])

        gmat = _dot_nt(jnp.concatenate([at_s, rt_s], axis=0), jnp.concatenate([bt_s, kt_s], axis=0), precision=HIGHEST)
        n2 = RWKV_PAIR
        a_ab = gmat[:n2, :n2] * strict
        a_ak = gmat[:n2, n2:] * strict
        a_rb = gmat[n2:, :n2] * incl
        a_rk = gmat[n2:, n2:] * incl

        tinv = eye + a_ab
        pw = a_ab
        span = 2
        while span < c:
            pw = hdot(pw, pw)
            tinv = tinv + hdot(tinv, pw)
            span *= 2

        wu = hdot(tinv, jnp.concatenate([at_s, hdot(a_ak, v_s)], axis=1))
        rb = hdot(a_rb, wu)
        rq_s = rt_s + rb[:, :n2]
        y0_s = rb[:, n2:] + hdot(a_rk, v_s)
        bw = hdot(bend_s.T, wu)
        mc = eye * jnp.exp(cle) + bw[:, :n2]
        n0 = bw[:, n2:] + hdot(kend_s.T, v_s)

        st = st_ref[...]
        y_s = hdot(rq_s, st) + y0_s
        st_ref[...] = hdot(mc, st) + n0
        ys.append(y_s[:c] + y_s[c:])
    y = jnp.concatenate(ys, axis=0)

    mean = head_sum(y) * (1.0 / hd)
    yc = y - mean
    var = head_sum(yc * yc) * (1.0 / hd)
    yn = yc * lax.rsqrt(var + RWKV_LNX_EPS) * lg_ref[...] + lb_ref[...]
    bonus = head_sum(r * k2 * rk_ref[...]) * v
    o_ref[...] = ((yn + bonus) * g_ref[...]).astype(o_ref.dtype)


def _rwkv_core(rkv, w_pre, a_pre, g, w0, a0, k_k, k_a, r_k, lnx_g, lnx_b, bsz, seqlen, tb=256):
    _, n, d = rkv.shape
    tb = min(tb, seqlen)
    nt = seqlen // tb
    npair = d // RWKV_PAIR
    rkv_spec = lambda j: pl.BlockSpec((1, tb, RWKV_PAIR), lambda b, p, t: (j, b * nt + t, p))
    act_spec = pl.BlockSpec((tb, RWKV_PAIR), lambda b, p, t: (b * nt + t, p))
    row_spec = pl.BlockSpec((1, RWKV_PAIR), lambda b, p, t: (0, p))
    return pl.pallas_call(
        functools.partial(_rwkv_core_kernel, tb=tb),
        grid=(bsz, npair, nt),
        in_specs=[rkv_spec(0), rkv_spec(1), rkv_spec(2), act_spec, act_spec, act_spec] + [row_spec] * 7,
        out_specs=act_spec,
        out_shape=jax.ShapeDtypeStruct((n, d), BF16),
        scratch_shapes=[pltpu.VMEM((RWKV_PAIR, RWKV_PAIR), F32)],
        compiler_params=_params(3),
        name="rwkv_core",
    )(rkv, rkv, rkv, w_pre, a_pre, g, w0, a0, k_k, k_a, r_k, lnx_g, lnx_b)


def _pool_kernel(x_ref, prev_ref, w_ref, sc_ref, g_ref, b_ref, o_ref, *, tm, seqlen, alpha):
    start = (pl.program_id(0) * tm) % seqlen
    x = x_ref[...]
    prev = jnp.where(start == 0, 0.0, prev_ref[...])
    xe = jnp.concatenate([prev, x], axis=0)
    pos = start + lax.broadcasted_iota(jnp.int32, (tm, 1), 0)
    gw = w_ref.shape[1]
    ys = []
    for gi, win in enumerate(POOL_WINDOWS):
        sl = slice(gi * gw, (gi + 1) * gw)
        s = xe[:, sl]
        d = 1
        while d < win:
            s = s + pltpu.roll(s, d, axis=0)
            d *= 2
        cnt = jnp.minimum(pos + 1, win).astype(F32)
        pooled = s[POOL_HALO:] / cnt - x[:, sl]
        ys.append(_dot(pooled.astype(BF16), w_ref[gi]))
    y = jnp.concatenate(ys, axis=1) * sc_ref[...]
    o_ref[...] = _layer_norm_rows(alpha * x + y, g_ref[...], b_ref[...])


def _pool_ln(x, w_pool, scale, g, b, seqlen, alpha, tm=512):
    n, d = x.shape
    tm = min(tm, seqlen)
    step = tm // POOL_HALO
    row_spec = pl.BlockSpec((1, d), lambda i: (0, 0))
    return pl.pallas_call(
        functools.partial(_pool_kernel, tm=tm, seqlen=seqlen, alpha=alpha),
        grid=(n // tm,),
        in_specs=[
            pl.BlockSpec((tm, d), lambda i: (i, 0)),
            pl.BlockSpec((POOL_HALO, d), lambda i: (jnp.maximum(i * step - 1, 0), 0)),
            pl.BlockSpec(w_pool.shape, lambda i: (0, 0, 0)),
            row_spec,
            row_spec,
            row_spec,
        ],
        out_specs=pl.BlockSpec((tm, d), lambda i: (i, 0)),
        out_shape=jax.ShapeDtypeStruct((n, d), F32),
        compiler_params=_params(1),
        name="pool_ln",
    )(x, x, w_pool, scale, g, b)


def kernel(x, ln_g, ln_b, mlp_w1, mlp_w2, rg_w_in, rg_conv_w, rg_conv_b, rg_gate_a_w, rg_gate_a_b, rg_gate_x_w, rg_gate_x_b, rg_lambda, rg_w_out, moba_w_qkv, moba_w_out, rwkv_mu, rwkv_w_rkv, rwkv_w0, rwkv_w_w1, rwkv_w_w2, rwkv_a0, rwkv_a_w1, rwkv_a_w2, rwkv_g_w1, rwkv_g_w2, rwkv_k_k, rwkv_k_a, rwkv_r_k, rwkv_lnx_g, rwkv_lnx_b, rwkv_w_out, pool_w, pool_scale):
    bsz, seqlen, d = x.shape
    depth = ln_g.shape[0]
    n_mixers = 4
    alpha = (2.0 * depth) ** 0.25
    bf = lambda w: w.astype(BF16)
    row = lambda p: p.reshape(1, -1)

    h = x.reshape(bsz * seqlen, d)
    counts = [0] * n_mixers
    for layer in range(depth):
        m = layer % n_mixers
        j = counts[m]
        counts[m] += 1
        g0, b0 = row(ln_g[layer, 0]), row(ln_b[layer, 0])
        if m == 0:
            proj = _matmul(h, bf(rg_w_in[j]), F32)
            mixed = _rglru(proj, rg_conv_w[j], row(rg_conv_b[j]), bf(rg_gate_a_w[j]), row(rg_gate_a_b[j]),
                           bf(rg_gate_x_w[j]), row(rg_gate_x_b[j]), row(rg_lambda[j]), bsz, seqlen)
            h = _matmul_ln(mixed, bf(rg_w_out[j]), h, g0, b0, alpha)
        elif m == 1:
            qkv = _qkv_rope(h, bf(moba_w_qkv[j]), seqlen)
            att = _moba_attention(qkv, bsz, seqlen)
            h = _matmul_ln(att, bf(moba_w_out[j]), h, g0, b0, alpha)
        elif m == 2:
            rkv = _rwkv_proj(h, rwkv_mu[j], bf(rwkv_w_rkv[j]), seqlen)
            w_pre, a_pre, gate = _rwkv_lora(h, rwkv_mu[j], bf(rwkv_w_w1[j]), bf(rwkv_w_w2[j]), bf(rwkv_a_w1[j]),
                                            bf(rwkv_a_w2[j]), bf(rwkv_g_w1[j]), bf(rwkv_g_w2[j]), seqlen)
            mixed = _rwkv_core(rkv, w_pre, a_pre, gate, row(rwkv_w0[j]), row(rwkv_a0[j]), row(rwkv_k_k[j]), row(rwkv_k_a[j]),
                               row(rwkv_r_k[j]), row(rwkv_lnx_g[j]), row(rwkv_lnx_b[j]), bsz, seqlen)
            h = _matmul_ln(mixed, bf(rwkv_w_out[j]), h, g0, b0, alpha)
        else:
            h = _pool_ln(h, bf(pool_w[j]), row(pool_scale[j]), g0, b0, seqlen, alpha)
        h = _mlp_ln(h, bf(mlp_w1[layer]), bf(mlp_w2[layer]), row(ln_g[layer, 1]), row(ln_b[layer, 1]), alpha)
    return h.reshape(bsz, seqlen, d)
```

```python
import functools

import jax
import jax.numpy as jnp
from jax import lax
from jax.experimental import pallas as pl
from jax.experimental.pallas import tpu as pltpu

F32 = jnp.float32
BF16 = jnp.bfloat16
HIGHEST = lax.Precision.HIGHEST

LN_EPS = 1e-5
NEG_INF = -1e30
VMEM_LIMIT_BYTES = 48 * 1024 * 1024

LRU_C = 8.0
CONV_WIDTH = 4
ATTN_HEAD_DIM = 128
MOBA_BLOCK = 256
MOBA_TOPK = 3
ROPE_THETA = 500000.0
ROPE_DIM = ATTN_HEAD_DIM // 4
RWKV_HEAD_DIM = 64
RWKV_PAIR = 2 * RWKV_HEAD_DIM
RWKV_CHUNK = 64
RWKV_LNX_EPS = 64e-5
POOL_WINDOWS = (2, 4, 8, 16)
POOL_HALO = 16
LANES = 128
SUBLANES = 8


def _params(n_axes):
    return pltpu.CompilerParams(dimension_semantics=("arbitrary",) * n_axes, vmem_limit_bytes=VMEM_LIMIT_BYTES)


def _layer_norm_rows(z, g, b):
    mean = jnp.mean(z, axis=-1, keepdims=True)
    zc = z - mean
    var = jnp.mean(zc * zc, axis=-1, keepdims=True)
    return zc * lax.rsqrt(var + LN_EPS) * g + b


def _softplus(z):
    return jnp.maximum(z, 0.0) + jnp.log1p(jnp.exp(-jnp.abs(z)))


def _dot(a, b, precision=None):
    return jnp.dot(a, b, preferred_element_type=F32, precision=precision)


def _split_bf16(z):
    hi = z.astype(BF16)
    return hi, (z - hi.astype(F32)).astype(BF16)


def _dot_bf16x3(a, b):
    a_hi, a_lo = _split_bf16(a)
    b_hi, b_lo = _split_bf16(b)
    return _dot(a_hi, b_hi) + (_dot(a_lo, b_hi) + _dot(a_hi, b_lo))


def _dot_nt(a, b, precision=None):
    return lax.dot_general(a, b, (((1,), (1,)), ((), ())), preferred_element_type=F32, precision=precision)


def _mm_kernel(a_ref, w_ref, o_ref):
    o_ref[...] = _dot(a_ref[...].astype(BF16), w_ref[...]).astype(o_ref.dtype)


def _matmul(a, w, out_dtype, tm=512, tn=1024):
    n, k = a.shape
    nout = w.shape[1]
    tn = min(tn, nout)
    return pl.pallas_call(
        _mm_kernel,
        grid=(nout // tn, n // tm),
        in_specs=[pl.BlockSpec((tm, k), lambda j, i: (i, 0)), pl.BlockSpec((k, tn), lambda j, i: (0, j))],
        out_specs=pl.BlockSpec((tm, tn), lambda j, i: (i, j)),
        out_shape=jax.ShapeDtypeStruct((n, nout), out_dtype),
        compiler_params=_params(2),
        name="matmul",
    )(a, w)


def _mm_ln_kernel(a_ref, w_ref, x_ref, g_ref, b_ref, o_ref, *, alpha):
    y = _dot(a_ref[...].astype(BF16), w_ref[...])
    o_ref[...] = _layer_norm_rows(alpha * x_ref[...] + y, g_ref[...], b_ref[...])


def _matmul_ln(a, w, x, g, b, alpha, tm=512):
    n, k = a.shape
    d = w.shape[1]
    return pl.pallas_call(
        functools.partial(_mm_ln_kernel, alpha=alpha),
        grid=(n // tm,),
        in_specs=[
            pl.BlockSpec((tm, k), lambda i: (i, 0)),
            pl.BlockSpec((k, d), lambda i: (0, 0)),
            pl.BlockSpec((tm, d), lambda i: (i, 0)),
            pl.BlockSpec((1, d), lambda i: (0, 0)),
            pl.BlockSpec((1, d), lambda i: (0, 0)),
        ],
        out_specs=pl.BlockSpec((tm, d), lambda i: (i, 0)),
        out_shape=jax.ShapeDtypeStruct((n, d), F32),
        compiler_params=_params(1),
        name="matmul_ln",
    )(a, w, x, g, b)


def _mlp_kernel(x_ref, w1_ref, w2_ref, g_ref, b_ref, o_ref, xb_ref, acc_ref, *, alpha):
    f = pl.program_id(1)

    @pl.when(f == 0)
    def _():
        xb_ref[...] = x_ref[...].astype(BF16)
        acc_ref[...] = jnp.zeros_like(acc_ref)

    h = jnp.maximum(_dot(xb_ref[...], w1_ref[...]), 0.0)
    acc_ref[...] += _dot((h * h).astype(BF16), w2_ref[...])

    @pl.when(f == pl.num_programs(1) - 1)
    def _():
        o_ref[...] = _layer_norm_rows(alpha * x_ref[...] + acc_ref[...], g_ref[...], b_ref[...])


def _mlp_ln(x, w1, w2, g, b, alpha, tm=512, tf=1024):
    n, d = x.shape
    dff = w1.shape[1]
    return pl.pallas_call(
        functools.partial(_mlp_kernel, alpha=alpha),
        grid=(n // tm, dff // tf),
        in_specs=[
            pl.BlockSpec((tm, d), lambda i, f: (i, 0)),
            pl.BlockSpec((d, tf), lambda i, f: (0, f)),
            pl.BlockSpec((tf, d), lambda i, f: (f, 0)),
            pl.BlockSpec((1, d), lambda i, f: (0, 0)),
            pl.BlockSpec((1, d), lambda i, f: (0, 0)),
        ],
        out_specs=pl.BlockSpec((tm, d), lambda i, f: (i, 0)),
        out_shape=jax.ShapeDtypeStruct((n, d), F32),
        scratch_shapes=[pltpu.VMEM((tm, d), BF16), pltpu.VMEM((tm, d), F32)],
        compiler_params=_params(2),
        name="mlp_ln",
    )(x, w1, w2, g, b)


def _rglru_kernel(gate_ref, u_ref, cw_ref, cb_ref, aw_ref, ab_ref, xw_ref, xb_ref, lam_ref, o_ref, hc_ref, ut_ref, *, ts):
    @pl.when(pl.program_id(2) == 0)
    def _():
        hc_ref[...] = jnp.zeros_like(hc_ref)
        ut_ref[...] = jnp.zeros_like(ut_ref)

    u = u_ref[...]
    ue = jnp.concatenate([ut_ref[...], u], axis=0)
    conv = cw_ref[CONV_WIDTH - 1 : CONV_WIDTH, :] * u + cb_ref[...]
    for d in range(1, CONV_WIDTH):
        conv = conv + cw_ref[CONV_WIDTH - 1 - d : CONV_WIDTH - d, :] * pltpu.roll(ue, d, axis=0)[SUBLANES:]
    ut_ref[...] = u[ts - SUBLANES :, :]

    cb16 = conv.astype(BF16)
    r = jax.nn.sigmoid(_dot(cb16, aw_ref[0]) + ab_ref[...])
    i = jax.nn.sigmoid(_dot(cb16, xw_ref[0]) + xb_ref[...])
    log_a = (-LRU_C) * r * _softplus(-lam_ref[...])
    a = jnp.exp(log_a)
    b = conv * i * jnp.sqrt(1.0 - a * a)

    row = lax.broadcasted_iota(jnp.int32, a.shape, 0)
    d = 1
    while d < ts:
        keep = row >= d
        a_sh = jnp.where(keep, pltpu.roll(a, d, axis=0), 1.0)
        b_sh = jnp.where(keep, pltpu.roll(b, d, axis=0), 0.0)
        b = a * b_sh + b
        a = a * a_sh
        d *= 2
    h = a * hc_ref[0:1, :] + b
    hc_ref[...] = jnp.broadcast_to(h[ts - 1 : ts, :], hc_ref.shape)
    o_ref[...] = (jax.nn.gelu(gate_ref[...]) * h).astype(o_ref.dtype)


def _rglru(proj, conv_w, conv_b, gate_a_w, gate_a_b, gate_x_w, gate_x_b, lam, bsz, seqlen, ts=512):
    n, r2 = proj.shape
    r = r2 // 2
    nblk, w, _ = gate_a_w.shape
    ts = min(ts, seqlen)
    nt = seqlen // ts
    row_spec = pl.BlockSpec((1, w), lambda b, j, t: (0, j))
    gw_spec = pl.BlockSpec((1, w, w), lambda b, j, t: (j, 0, 0))
    return pl.pallas_call(
        functools.partial(_rglru_kernel, ts=ts),
        grid=(bsz, nblk, nt),
        in_specs=[
            pl.BlockSpec((ts, w), lambda b, j, t: (b * nt + t, j)),
            pl.BlockSpec((ts, w), lambda b, j, t: (b * nt + t, nblk + j)),
            pl.BlockSpec((CONV_WIDTH, w), lambda b, j, t: (0, j)),
            row_spec,
            gw_spec,
            row_spec,
            gw_spec,
            row_spec,
            row_spec,
        ],
        out_specs=pl.BlockSpec((ts, w), lambda b, j, t: (b * nt + t, j)),
        out_shape=jax.ShapeDtypeStruct((n, r), BF16),
        scratch_shapes=[pltpu.VMEM((SUBLANES, w), F32), pltpu.VMEM((SUBLANES, w), F32)],
        compiler_params=_params(3),
        name="rglru",
    )(proj, proj, conv_w, conv_b, gate_a_w, gate_a_b, gate_x_w, gate_x_b, lam)


def _qkv_kernel(x_ref, w_ref, c_ref, s1_ref, s2_ref, o_ref, *, tn, d_model, scale):
    acc = _dot(x_ref[...].astype(BF16), w_ref[...])
    rep = tn // ATTN_HEAD_DIM
    c = jnp.concatenate([c_ref[...]] * rep, axis=1)
    s1 = jnp.concatenate([s1_ref[...]] * rep, axis=1)
    s2 = jnp.concatenate([s2_ref[...]] * rep, axis=1)
    half = ROPE_DIM // 2
    roped = acc * c + pltpu.roll(acc, half, axis=1) * s1 + pltpu.roll(acc, tn - half, axis=1) * s2
    col0 = pl.program_id(0) * tn
    out = jnp.where(col0 >= 2 * d_model, acc, roped) * jnp.where(col0 < d_model, scale, 1.0)
    o_ref[...] = out.astype(o_ref.dtype)


def _rope_tables(seqlen):
    half = ROPE_DIM // 2
    pos = jnp.arange(seqlen, dtype=F32)
    inv_freq = ROPE_THETA ** (-jnp.arange(0, ROPE_DIM, 2, dtype=F32) / ROPE_DIM)
    ang = pos[:, None] * inv_freq[None, :]
    cos, sin = jnp.cos(ang), jnp.sin(ang)
    pad = ATTN_HEAD_DIM - ROPE_DIM
    c = jnp.concatenate([cos, cos, jnp.ones((seqlen, pad), F32)], axis=1)
    s1 = jnp.concatenate([jnp.zeros((seqlen, half), F32), sin, jnp.zeros((seqlen, pad), F32)], axis=1)
    s2 = jnp.concatenate([-sin, jnp.zeros((seqlen, half + pad), F32)], axis=1)
    return c, s1, s2


def _qkv_rope(x, w_qkv, seqlen, tm=512, tn=512):
    n, d = x.shape
    nout = w_qkv.shape[1]
    tm = min(tm, seqlen)
    tn = min(tn, d)
    c, s1, s2 = _rope_tables(seqlen)
    nt = seqlen // tm
    tab_spec = pl.BlockSpec((tm, ATTN_HEAD_DIM), lambda j, i: (i % nt, 0))
    return pl.pallas_call(
        functools.partial(_qkv_kernel, tn=tn, d_model=d, scale=ATTN_HEAD_DIM**-0.5),
        grid=(nout // tn, n // tm),
        in_specs=[pl.BlockSpec((tm, d), lambda j, i: (i, 0)), pl.BlockSpec((d, tn), lambda j, i: (0, j)), tab_spec, tab_spec, tab_spec],
        out_specs=pl.BlockSpec((tm, tn), lambda j, i: (i, j)),
        out_shape=jax.ShapeDtypeStruct((n, nout), BF16),
        compiler_params=_params(2),
        name="qkv_rope",
    )(x, w_qkv, c, s1, s2)


def _moba_kernel(q_ref, k_ref, v_ref, o_ref, kmean_ref, vt_ref, sel_ref, s_ref, *, nblk):
    i = pl.program_id(2)
    blk, hd = MOBA_BLOCK, ATTN_HEAD_DIM

    @pl.when(i == 0)
    def _():
        kf = k_ref[...].astype(F32).reshape(nblk, blk, hd)
        kmean_ref[...] = jnp.sum(kf, axis=1) * (1.0 / blk)
        vt_ref[...] = v_ref[...].astype(F32).T.astype(BF16)

    q = q_ref[...]

    def scores(jp):
        off = pl.multiple_of(jp * (2 * blk), 2 * blk)
        return _dot_nt(k_ref[pl.ds(off, 2 * blk), :], q)

    s_ref[0] = scores(0)

    gate = _dot_nt(kmean_ref[...], q.astype(F32), precision=HIGHEST)
    blk_id = lax.broadcasted_iota(jnp.int32, gate.shape, 0)
    past = jnp.where(blk_id < i, 1.0, 0.0)
    rows = []
    for n in range(nblk):
        gn = gate[n : n + 1, :]
        tie = jnp.where(blk_id < n, 1.0, 0.0)
        beats = jnp.where(gate > gn, 1.0, jnp.where(gate == gn, tie, 0.0)) * past
        cnt = jnp.sum(beats, axis=0, keepdims=True)
        rows.append(jnp.where(cnt < MOBA_TOPK, 1.0, 0.0) * jnp.where(n < i, 1.0, 0.0))
    sel_ref[...] = jnp.concatenate(rows, axis=0)

    off = pl.multiple_of(i * blk, blk)
    s = _dot_nt(k_ref[pl.ds(off, blk), :], q)
    key_pos = lax.broadcasted_iota(jnp.int32, s.shape, 0)
    qry_pos = lax.broadcasted_iota(jnp.int32, s.shape, 1)
    s = jnp.where(key_pos <= qry_pos, s, NEG_INF)
    m = jnp.max(s, axis=0, keepdims=True)
    p = jnp.exp(s - m)
    l = jnp.sum(p, axis=0, keepdims=True)
    acc = _dot(vt_ref[:, pl.ds(off, blk)], p.astype(BF16))

    def body(jp, carry):
        m, l, acc = carry
        off = pl.multiple_of(jp * (2 * blk), 2 * blk)
        s = s_ref[jp % 2]
        s_ref[(jp + 1) % 2] = scores(jnp.minimum(jp + 1, nblk // 2 - 1))
        s0 = jnp.where(sel_ref[pl.ds(2 * jp, 1), :] > 0.5, s[:blk], NEG_INF)
        s1 = jnp.where(sel_ref[pl.ds(2 * jp + 1, 1), :] > 0.5, s[blk:], NEG_INF)
        m_blk = jnp.maximum(jnp.max(s0, axis=0, keepdims=True), jnp.max(s1, axis=0, keepdims=True))
        m_new = jnp.maximum(m, m_blk)
        alpha = jnp.exp(m - m_new)
        p0 = jnp.exp(s0 - m_new)
        p1 = jnp.exp(s1 - m_new)
        l = alpha * l + jnp.sum(p0, axis=0, keepdims=True) + jnp.sum(p1, axis=0, keepdims=True)
        p = jnp.concatenate([p0.astype(BF16), p1.astype(BF16)], axis=0)
        acc = acc * alpha + _dot(vt_ref[:, pl.ds(off, 2 * blk)], p)
        return m_new, l, acc

    m, l, acc = lax.fori_loop(0, (i + 1) // 2, body, (m, l, acc))
    o_ref[...] = (acc / l).T.astype(o_ref.dtype)


def _moba_attention(qkv, bsz, seqlen):
    n, d3 = qkv.shape
    d = d3 // 3
    nheads = d // ATTN_HEAD_DIM
    nblk = seqlen // MOBA_BLOCK
    blk, hd = MOBA_BLOCK, ATTN_HEAD_DIM
    return pl.pallas_call(
        functools.partial(_moba_kernel, nblk=nblk),
        grid=(bsz, nheads, nblk),
        in_specs=[
            pl.BlockSpec((blk, hd), lambda b, h, i: (b * nblk + i, h)),
            pl.BlockSpec((seqlen, hd), lambda b, h, i: (b, nheads + h)),
            pl.BlockSpec((seqlen, hd), lambda b, h, i: (b, 2 * nheads + h)),
        ],
        out_specs=pl.BlockSpec((blk, hd), lambda b, h, i: (b * nblk + i, h)),
        out_shape=jax.ShapeDtypeStruct((n, d), BF16),
        scratch_shapes=[pltpu.VMEM((nblk, hd), F32), pltpu.VMEM((hd, seqlen), BF16), pltpu.VMEM((nblk, blk), F32),
                        pltpu.VMEM((2, 2 * blk, blk), F32)],
        compiler_params=_params(3),
        name="moba_attention",
    )(qkv, qkv, qkv)


def _token_shift_delta(x, prev_ref, is_start):
    prev_row = jnp.where(is_start, 0.0, prev_ref[SUBLANES - 1 : SUBLANES, :])
    row = lax.broadcasted_iota(jnp.int32, x.shape, 0)
    return jnp.where(row == 0, prev_row, pltpu.roll(x, 1, axis=0)) - x


def _rwkv_proj_kernel(x_ref, prev_ref, mu_ref, w_ref, o_ref, *, tm, seqlen):
    x = x_ref[...]
    xx = _token_shift_delta(x, prev_ref, (pl.program_id(1) * tm) % seqlen == 0)
    o_ref[0] = _dot((x + xx * mu_ref[0]).astype(BF16), w_ref[0])


def _rwkv_proj(x, mu, w_rkv, seqlen, tm=512):
    n, d = x.shape
    tm = min(tm, seqlen)
    step = tm // SUBLANES
    return pl.pallas_call(
        functools.partial(_rwkv_proj_kernel, tm=tm, seqlen=seqlen),
        grid=(3, n // tm),
        in_specs=[
            pl.BlockSpec((tm, d), lambda g, i: (i, 0)),
            pl.BlockSpec((SUBLANES, d), lambda g, i: (jnp.maximum(i * step - 1, 0), 0)),
            pl.BlockSpec((1, 1, d), lambda g, i: (g, 0, 0)),
            pl.BlockSpec((1, d, d), lambda g, i: (g, 0, 0)),
        ],
        out_specs=pl.BlockSpec((1, tm, d), lambda g, i: (g, i, 0)),
        out_shape=jax.ShapeDtypeStruct((3, n, d), F32),
        compiler_params=_params(2),
        name="rwkv_proj",
    )(x, x, mu.reshape(mu.shape[0], 1, d), w_rkv)


def _rwkv_lora_kernel(x_ref, prev_ref, mu_ref, ww1_ref, ww2_ref, aw1_ref, aw2_ref, gw1_ref, gw2_ref, wo_ref, ao_ref, go_ref, *, tm, seqlen):
    x = x_ref[...]
    xx = _token_shift_delta(x, prev_ref, (pl.program_id(0) * tm) % seqlen == 0)
    xw = (x + xx * mu_ref[3:4, :]).astype(BF16)
    xa = (x + xx * mu_ref[4:5, :]).astype(BF16)
    xg = (x + xx * mu_ref[5:6, :]).astype(BF16)
    wo_ref[...] = _dot(jnp.tanh(_dot(xw, ww1_ref[...])).astype(BF16), ww2_ref[...])
    ao_ref[...] = _dot(_dot(xa, aw1_ref[...]).astype(BF16), aw2_ref[...])
    go_ref[...] = _dot(jax.nn.sigmoid(_dot(xg, gw1_ref[...])).astype(BF16), gw2_ref[...])


def _pad_lora(w1, w2):
    rank = w1.shape[1]
    pad = (-rank) % LANES
    return jnp.pad(w1, ((0, 0), (0, pad))), jnp.pad(w2, ((0, pad), (0, 0)))


def _rwkv_lora(x, mu, w_w1, w_w2, a_w1, a_w2, g_w1, g_w2, seqlen, tm=256):
    n, d = x.shape
    tm = min(tm, seqlen)
    step = tm // SUBLANES
    ws = [*_pad_lora(w_w1, w_w2), *_pad_lora(a_w1, a_w2), *_pad_lora(g_w1, g_w2)]
    full = lambda a: pl.BlockSpec(a.shape, lambda i: (0, 0))
    out_spec = pl.BlockSpec((tm, d), lambda i: (i, 0))
    return pl.pallas_call(
        functools.partial(_rwkv_lora_kernel, tm=tm, seqlen=seqlen),
        grid=(n // tm,),
        in_specs=[
            pl.BlockSpec((tm, d), lambda i: (i, 0)),
            pl.BlockSpec((SUBLANES, d), lambda i: (jnp.maximum(i * step - 1, 0), 0)),
            full(mu),
            *[full(w) for w in ws],
        ],
        out_specs=[out_spec, out_spec, out_spec],
        out_shape=[jax.ShapeDtypeStruct((n, d), F32)] * 3,
        compiler_params=_params(1),
        name="rwkv_lora",
    )(x, x, mu, *ws)


def _rwkv_core_kernel(r_ref, k_ref, v_ref, wpre_ref, apre_ref, g_ref, w0_ref, a0_ref, kk_ref, ka_ref, rk_ref, lg_ref, lb_ref,
                      o_ref, st_ref, *, tb, npp):
    c = RWKV_CHUNK
    hd = RWKV_HEAD_DIM

    @pl.when(pl.program_id(2) == 0)
    def _():
        st_ref[...] = jnp.zeros_like(st_ref)

    lane = lax.broadcasted_iota(jnp.int32, (1, RWKV_PAIR), 1)
    m0 = jnp.where(lane < hd, 1.0, 0.0)
    m1 = 1.0 - m0
    ri = lax.broadcasted_iota(jnp.int32, (RWKV_PAIR, RWKV_PAIR), 0)
    ci = lax.broadcasted_iota(jnp.int32, (RWKV_PAIR, RWKV_PAIR), 1)
    same = (ri < hd) == (ci < hd)
    ones_bd = jnp.where(same, 1.0, 0.0)
    strict = jnp.where(same & (ci < ri), 1.0, 0.0)
    incl = jnp.where(same & (ci <= ri), 1.0, 0.0)
    eye = jnp.where(ri == ci, 1.0, 0.0)

    def head_sum(z):
        hi, lo = _split_bf16(z)
        return _dot(hi, ones_b16) + _dot(lo, ones_b16)

    def stack(z):
        return jnp.concatenate([z * m0, z * m1], axis=0)

    def b16(z):
        return z.astype(BF16)

    ones_b16 = b16(ones_bd)
    zeros_sq = jnp.zeros((RWKV_PAIR, RWKV_PAIR), F32)

    n2 = RWKV_PAIR
    nch = tb // c
    rowc = lax.broadcasted_iota(jnp.int32, (tb, n2), 0) % c

    prep = []
    for p in range(npp):
        ln = slice(p * n2, (p + 1) * n2)
        r = r_ref[0, :, ln]
        k = k_ref[0, :, ln]
        v = v_ref[0, :, ln]
        w_log = -_softplus(-(w0_ref[:, ln] + wpre_ref[:, ln])) - 0.5
        lw = -jnp.exp(w_log)
        a = jax.nn.sigmoid(a0_ref[:, ln] + apre_ref[:, ln])
        kk = k * kk_ref[:, ln]
        kk = kk / jnp.maximum(jnp.sqrt(head_sum(kk * kk)), 1e-12)
        k2 = k * (1.0 + (a - 1.0) * ka_ref[:, ln])
        bv = kk * a
        cl = lw
        d = 1
        while d < c:
            cl = cl + jnp.where(rowc >= d, pltpu.roll(cl, d, axis=0), 0.0)
            d *= 2
        prep.append((r, k2, v, kk, bv, lw, cl))

    items = [(p, ch) for p in range(npp) for ch in range(nch)]
    rt, at16, v16, lhs16, diag, a_ab, a_ak16 = {}, {}, {}, {}, {}, {}, {}
    for it in items:
        p, ch = it
        r, k2, v, kk, bv, lw, cl = prep[p]
        sl = slice(ch * c, (ch + 1) * c)
        clc = cl[sl]
        cle = clc[c - 1 : c, :]
        e_neg = jnp.exp(-clc)
        e_end = jnp.exp(cle - clc)
        rt_s = stack(r[sl] * jnp.exp(clc))
        at_s = stack(-kk[sl] * jnp.exp(clc - lw[sl]))
        gmat = _dot_nt(b16(jnp.concatenate([at_s, rt_s], axis=0)),
                       b16(jnp.concatenate([stack(bv[sl] * e_neg), stack(k2[sl] * e_neg)], axis=0)))
        a_ab[it] = gmat[:n2, :n2] * strict
        a_ak16[it] = b16(gmat[:n2, n2:] * strict)
        ends_t = jnp.concatenate([stack(bv[sl] * e_end).T, stack(k2[sl] * e_end).T], axis=1)
        lhs16[it] = b16(jnp.concatenate([gmat[n2:] * jnp.concatenate([incl, incl], axis=1), ends_t], axis=0))
        rt[it] = rt_s
        at16[it] = b16(at_s)
        v16[it] = b16(stack(v[sl]))
        diag[it] = eye * jnp.exp(cle)

    av16 = {it: b16(_dot(a_ak16[it], v16[it])) for it in items}
    tinv = {it: eye + a_ab[it] for it in items}
    pw = {it: _dot(b16(a_ab[it]), b16(a_ab[it])) for it in items}
    span = 2
    while span < c:
        span *= 2
        for it in items:
            if span < c:
                res = _dot(b16(pw[it]), b16(jnp.concatenate([tinv[it], pw[it]], axis=1)))
                tinv[it] = tinv[it] + res[:, :n2]
                pw[it] = res[:, n2:]
            else:
                tinv[it] = tinv[it] + _dot(b16(pw[it]), b16(tinv[it]))

    big = {}
    for it in items:
        wu16 = b16(_dot(b16(tinv[it]), jnp.concatenate([at16[it], av16[it]], axis=1)))
        rhs16 = jnp.concatenate([wu16, jnp.concatenate([b16(zeros_sq), v16[it]], axis=1)], axis=0)
        big[it] = _dot(lhs16[it], rhs16)

    st = {p: st_ref[p] for p in range(npp)}
    st_in = {}
    for ch in range(nch):
        for p in range(npp):
            it = (p, ch)
            st_in[it] = st[p]
            st[p] = _dot_bf16x3(diag[it] + big[it][n2:, :n2], st[p]) + big[it][n2:, n2:]
    for p in range(npp):
        st_ref[p] = st[p]

    for p in range(npp):
        ln = slice(p * n2, (p + 1) * n2)
        r, k2, v, kk, bv, lw, cl = prep[p]
        ys = []
        for ch in range(nch):
            it = (p, ch)
            y_s = _dot(b16(rt[it] + big[it][:n2, :n2]), b16(st_in[it])) + big[it][:n2, n2:]
            ys.append(y_s[:c] + y_s[c:])
        y = jnp.concatenate(ys, axis=0)
        mean = head_sum(y) * (1.0 / hd)
        yc = y - mean
        var = head_sum(yc * yc) * (1.0 / hd)
        yn = yc * lax.rsqrt(var + RWKV_LNX_EPS) * lg_ref[:, ln] + lb_ref[:, ln]
        bonus = head_sum(r * k2 * rk_ref[:, ln]) * v
        o_ref[:, ln] = ((yn + bonus) * g_ref[:, ln]).astype(o_ref.dtype)


def _rwkv_core(rkv, w_pre, a_pre, g, w0, a0, k_k, k_a, r_k, lnx_g, lnx_b, bsz, seqlen, tb=512, npp=2):
    _, n, d = rkv.shape
    tb = min(tb, seqlen)
    nt = seqlen // tb
    wl = npp * RWKV_PAIR
    rkv_spec = lambda j: pl.BlockSpec((1, tb, wl), lambda b, p, t: (j, b * nt + t, p))
    act_spec = pl.BlockSpec((tb, wl), lambda b, p, t: (b * nt + t, p))
    row_spec = pl.BlockSpec((1, wl), lambda b, p, t: (0, p))
    return pl.pallas_call(
        functools.partial(_rwkv_core_kernel, tb=tb, npp=npp),
        grid=(bsz, d // wl, nt),
        in_specs=[rkv_spec(0), rkv_spec(1), rkv_spec(2), act_spec, act_spec, act_spec] + [row_spec] * 7,
        out_specs=act_spec,
        out_shape=jax.ShapeDtypeStruct((n, d), BF16),
        scratch_shapes=[pltpu.VMEM((npp, RWKV_PAIR, RWKV_PAIR), F32)],
        compiler_params=_params(3),
        name="rwkv_core",
    )(rkv, rkv, rkv, w_pre, a_pre, g, w0, a0, k_k, k_a, r_k, lnx_g, lnx_b)


def _pool_kernel(x_ref, prev_ref, w_ref, sc_ref, g_ref, b_ref, o_ref, *, tm, seqlen, alpha):
    start = (pl.program_id(0) * tm) % seqlen
    x = x_ref[...]
    prev = jnp.where(start == 0, 0.0, prev_ref[...])
    xe = jnp.concatenate([prev, x], axis=0)
    pos = start + lax.broadcasted_iota(jnp.int32, (tm, 1), 0)
    gw = w_ref.shape[1]
    ys = []
    for gi, win in enumerate(POOL_WINDOWS):
        sl = slice(gi * gw, (gi + 1) * gw)
        s = xe[:, sl]
        d = 1
        while d < win:
            s = s + pltpu.roll(s, d, axis=0)
            d *= 2
        cnt = jnp.minimum(pos + 1, win).astype(F32)
        pooled = s[POOL_HALO:] / cnt - x[:, sl]
        ys.append(_dot(pooled.astype(BF16), w_ref[gi]))
    y = jnp.concatenate(ys, axis=1) * sc_ref[...]
    o_ref[...] = _layer_norm_rows(alpha * x + y, g_ref[...], b_ref[...])


def _pool_ln(x, w_pool, scale, g, b, seqlen, alpha, tm=512):
    n, d = x.shape
    tm = min(tm, seqlen)
    step = tm // POOL_HALO
    row_spec = pl.BlockSpec((1, d), lambda i: (0, 0))
    return pl.pallas_call(
        functools.partial(_pool_kernel, tm=tm, seqlen=seqlen, alpha=alpha),
        grid=(n // tm,),
        in_specs=[
            pl.BlockSpec((tm, d), lambda i: (i, 0)),
            pl.BlockSpec((POOL_HALO, d), lambda i: (jnp.maximum(i * step - 1, 0), 0)),
            pl.BlockSpec(w_pool.shape, lambda i: (0, 0, 0)),
            row_spec,
            row_spec,
            row_spec,
        ],
        out_specs=pl.BlockSpec((tm, d), lambda i: (i, 0)),
        out_shape=jax.ShapeDtypeStruct((n, d), F32),
        compiler_params=_params(1),
        name="pool_ln",
    )(x, x, w_pool, scale, g, b)


def kernel(x, ln_g, ln_b, mlp_w1, mlp_w2, rg_w_in, rg_conv_w, rg_conv_b, rg_gate_a_w, rg_gate_a_b, rg_gate_x_w, rg_gate_x_b, rg_lambda, rg_w_out, moba_w_qkv, moba_w_out, rwkv_mu, rwkv_w_rkv, rwkv_w0, rwkv_w_w1, rwkv_w_w2, rwkv_a0, rwkv_a_w1, rwkv_a_w2, rwkv_g_w1, rwkv_g_w2, rwkv_k_k, rwkv_k_a, rwkv_r_k, rwkv_lnx_g, rwkv_lnx_b, rwkv_w_out, pool_w, pool_scale):
    bsz, seqlen, d = x.shape
    depth = ln_g.shape[0]
    n_mixers = 4
    alpha = (2.0 * depth) ** 0.25
    bf = lambda w: w.astype(BF16)
    row = lambda p: p.reshape(1, -1)

    h = x.reshape(bsz * seqlen, d)
    counts = [0] * n_mixers
    for layer in range(depth):
        m = layer % n_mixers
        j = counts[m]
        counts[m] += 1
        g0, b0 = row(ln_g[layer, 0]), row(ln_b[layer, 0])
        if m == 0:
            proj = _matmul(h, bf(rg_w_in[j]), F32)
            mixed = _rglru(proj, rg_conv_w[j], row(rg_conv_b[j]), bf(rg_gate_a_w[j]), row(rg_gate_a_b[j]),
                           bf(rg_gate_x_w[j]), row(rg_gate_x_b[j]), row(rg_lambda[j]), bsz, seqlen)
            h = _matmul_ln(mixed, bf(rg_w_out[j]), h, g0, b0, alpha)
        elif m == 1:
            qkv = _qkv_rope(h, bf(moba_w_qkv[j]), seqlen)
            att = _moba_attention(qkv, bsz, seqlen)
            h = _matmul_ln(att, bf(moba_w_out[j]), h, g0, b0, alpha)
        elif m == 2:
            rkv = _rwkv_proj(h, rwkv_mu[j], bf(rwkv_w_rkv[j]), seqlen)
            w_pre, a_pre, gate = _rwkv_lora(h, rwkv_mu[j], bf(rwkv_w_w1[j]), bf(rwkv_w_w2[j]), bf(rwkv_a_w1[j]),
                                            bf(rwkv_a_w2[j]), bf(rwkv_g_w1[j]), bf(rwkv_g_w2[j]), seqlen)
            mixed = _rwkv_core(rkv, w_pre, a_pre, gate, row(rwkv_w0[j]), row(rwkv_a0[j]), row(rwkv_k_k[j]), row(rwkv_k_a[j]),
                               row(rwkv_r_k[j]), row(rwkv_lnx_g[j]), row(rwkv_lnx_b[j]), bsz, seqlen)
            h = _matmul_ln(mixed, bf(rwkv_w_out[j]), h, g0, b0, alpha)
        else:
            h = _pool_ln(h, bf(pool_w[j]), row(pool_scale[j]), g0, b0, seqlen, alpha)
        h = _mlp_ln(h, bf(mlp_w1[layer]), bf(mlp_w2[layer]), row(ln_g[layer, 1]), row(ln_b[layer, 1]), alpha)
    return h.reshape(bsz, seqlen, d)
```

```python
import functools

import jax
import jax.numpy as jnp
from jax import lax
from jax.experimental import pallas as pl
from jax.experimental.pallas import tpu as pltpu

F32 = jnp.float32
BF16 = jnp.bfloat16
HIGHEST = lax.Precision.HIGHEST

LN_EPS = 1e-5
NEG_INF = -1e30
VMEM_LIMIT_BYTES = 48 * 1024 * 1024

LRU_C = 8.0
CONV_WIDTH = 4
ATTN_HEAD_DIM = 128
MOBA_BLOCK = 256
MOBA_TOPK = 3
ROPE_THETA = 500000.0
ROPE_DIM = ATTN_HEAD_DIM // 4
RWKV_HEAD_DIM = 64
RWKV_PAIR = 2 * RWKV_HEAD_DIM
RWKV_CHUNK = 64
RWKV_LNX_EPS = 64e-5
POOL_WINDOWS = (2, 4, 8, 16)
POOL_HALO = 16
LANES = 128
SUBLANES = 8


def _params(n_axes):
    return pltpu.CompilerParams(dimension_semantics=("arbitrary",) * n_axes, vmem_limit_bytes=VMEM_LIMIT_BYTES)


def _layer_norm_rows(z, g, b):
    mean = jnp.mean(z, axis=-1, keepdims=True)
    zc = z - mean
    var = jnp.mean(zc * zc, axis=-1, keepdims=True)
    return zc * lax.rsqrt(var + LN_EPS) * g + b


def _softplus(z):
    return jnp.maximum(z, 0.0) + jnp.log1p(jnp.exp(-jnp.abs(z)))


def _dot(a, b, precision=None):
    return jnp.dot(a, b, preferred_element_type=F32, precision=precision)


def _split_bf16(z):
    hi = z.astype(BF16)
    return hi, (z - hi.astype(F32)).astype(BF16)


def _dot_bf16x3(a, b):
    a_hi, a_lo = _split_bf16(a)
    b_hi, b_lo = _split_bf16(b)
    return _dot(jnp.concatenate([a_hi, a_lo], axis=1), jnp.concatenate([b_hi, b_hi], axis=0)) + _dot(a_hi, b_lo)


def _dot_nt(a, b, precision=None):
    return lax.dot_general(a, b, (((1,), (1,)), ((), ())), preferred_element_type=F32, precision=precision)


def _mm_kernel(a_ref, w_ref, o_ref):
    o_ref[...] = _dot(a_ref[...].astype(BF16), w_ref[...]).astype(o_ref.dtype)


def _matmul(a, w, out_dtype, tm=512, tn=1024):
    n, k = a.shape
    nout = w.shape[1]
    tn = min(tn, nout)
    return pl.pallas_call(
        _mm_kernel,
        grid=(nout // tn, n // tm),
        in_specs=[pl.BlockSpec((tm, k), lambda j, i: (i, 0)), pl.BlockSpec((k, tn), lambda j, i: (0, j))],
        out_specs=pl.BlockSpec((tm, tn), lambda j, i: (i, j)),
        out_shape=jax.ShapeDtypeStruct((n, nout), out_dtype),
        compiler_params=_params(2),
        name="matmul",
    )(a, w)


def _mm_ln_kernel(a_ref, w_ref, x_ref, g_ref, b_ref, o_ref, *, alpha):
    y = _dot(a_ref[...].astype(BF16), w_ref[...])
    o_ref[...] = _layer_norm_rows(alpha * x_ref[...] + y, g_ref[...], b_ref[...])


def _matmul_ln(a, w, x, g, b, alpha, tm=512):
    n, k = a.shape
    d = w.shape[1]
    return pl.pallas_call(
        functools.partial(_mm_ln_kernel, alpha=alpha),
        grid=(n // tm,),
        in_specs=[
            pl.BlockSpec((tm, k), lambda i: (i, 0)),
            pl.BlockSpec((k, d), lambda i: (0, 0)),
            pl.BlockSpec((tm, d), lambda i: (i, 0)),
            pl.BlockSpec((1, d), lambda i: (0, 0)),
            pl.BlockSpec((1, d), lambda i: (0, 0)),
        ],
        out_specs=pl.BlockSpec((tm, d), lambda i: (i, 0)),
        out_shape=jax.ShapeDtypeStruct((n, d), F32),
        compiler_params=_params(1),
        name="matmul_ln",
    )(a, w, x, g, b)


def _mlp_kernel(x_ref, w1_ref, w2_ref, g_ref, b_ref, o_ref, xb_ref, acc_ref, *, alpha):
    f = pl.program_id(1)

    @pl.when(f == 0)
    def _():
        xb_ref[...] = x_ref[...].astype(BF16)
        acc_ref[...] = jnp.zeros_like(acc_ref)

    h = jnp.maximum(_dot(xb_ref[...], w1_ref[...]), 0.0)
    acc_ref[...] += _dot((h * h).astype(BF16), w2_ref[...])

    @pl.when(f == pl.num_programs(1) - 1)
    def _():
        o_ref[...] = _layer_norm_rows(alpha * x_ref[...] + acc_ref[...], g_ref[...], b_ref[...])


def _mlp_ln(x, w1, w2, g, b, alpha, tm=512, tf=1024):
    n, d = x.shape
    dff = w1.shape[1]
    return pl.pallas_call(
        functools.partial(_mlp_kernel, alpha=alpha),
        grid=(n // tm, dff // tf),
        in_specs=[
            pl.BlockSpec((tm, d), lambda i, f: (i, 0)),
            pl.BlockSpec((d, tf), lambda i, f: (0, f)),
            pl.BlockSpec((tf, d), lambda i, f: (f, 0)),
            pl.BlockSpec((1, d), lambda i, f: (0, 0)),
            pl.BlockSpec((1, d), lambda i, f: (0, 0)),
        ],
        out_specs=pl.BlockSpec((tm, d), lambda i, f: (i, 0)),
        out_shape=jax.ShapeDtypeStruct((n, d), F32),
        scratch_shapes=[pltpu.VMEM((tm, d), BF16), pltpu.VMEM((tm, d), F32)],
        compiler_params=_params(2),
        name="mlp_ln",
    )(x, w1, w2, g, b)


def _rglru_kernel(gate_ref, u_ref, cw_ref, cb_ref, aw_ref, ab_ref, xw_ref, xb_ref, lam_ref, o_ref, hc_ref, ut_ref, *, ts):
    @pl.when(pl.program_id(2) == 0)
    def _():
        hc_ref[...] = jnp.zeros_like(hc_ref)
        ut_ref[...] = jnp.zeros_like(ut_ref)

    u = u_ref[...]
    ue = jnp.concatenate([ut_ref[...], u], axis=0)
    conv = cw_ref[CONV_WIDTH - 1 : CONV_WIDTH, :] * u + cb_ref[...]
    for d in range(1, CONV_WIDTH):
        conv = conv + cw_ref[CONV_WIDTH - 1 - d : CONV_WIDTH - d, :] * pltpu.roll(ue, d, axis=0)[SUBLANES:]
    ut_ref[...] = u[ts - SUBLANES :, :]

    cb16 = conv.astype(BF16)
    r = jax.nn.sigmoid(_dot(cb16, aw_ref[0]) + ab_ref[...])
    i = jax.nn.sigmoid(_dot(cb16, xw_ref[0]) + xb_ref[...])
    log_a = (-LRU_C) * r * _softplus(-lam_ref[...])
    a = jnp.exp(log_a)
    b = conv * i * jnp.sqrt(1.0 - a * a)

    row = lax.broadcasted_iota(jnp.int32, a.shape, 0)
    d = 1
    while d < ts:
        keep = row >= d
        a_sh = jnp.where(keep, pltpu.roll(a, d, axis=0), 1.0)
        b_sh = jnp.where(keep, pltpu.roll(b, d, axis=0), 0.0)
        b = a * b_sh + b
        a = a * a_sh
        d *= 2
    h = a * hc_ref[0:1, :] + b
    hc_ref[...] = jnp.broadcast_to(h[ts - 1 : ts, :], hc_ref.shape)
    o_ref[...] = (jax.nn.gelu(gate_ref[...]) * h).astype(o_ref.dtype)


def _rglru(proj, conv_w, conv_b, gate_a_w, gate_a_b, gate_x_w, gate_x_b, lam, bsz, seqlen, ts=512):
    n, r2 = proj.shape
    r = r2 // 2
    nblk, w, _ = gate_a_w.shape
    ts = min(ts, seqlen)
    nt = seqlen // ts
    row_spec = pl.BlockSpec((1, w), lambda b, j, t: (0, j))
    gw_spec = pl.BlockSpec((1, w, w), lambda b, j, t: (j, 0, 0))
    return pl.pallas_call(
        functools.partial(_rglru_kernel, ts=ts),
        grid=(bsz, nblk, nt),
        in_specs=[
            pl.BlockSpec((ts, w), lambda b, j, t: (b * nt + t, j)),
            pl.BlockSpec((ts, w), lambda b, j, t: (b * nt + t, nblk + j)),
            pl.BlockSpec((CONV_WIDTH, w), lambda b, j, t: (0, j)),
            row_spec,
            gw_spec,
            row_spec,
            gw_spec,
            row_spec,
            row_spec,
        ],
        out_specs=pl.BlockSpec((ts, w), lambda b, j, t: (b * nt + t, j)),
        out_shape=jax.ShapeDtypeStruct((n, r), BF16),
        scratch_shapes=[pltpu.VMEM((SUBLANES, w), F32), pltpu.VMEM((SUBLANES, w), F32)],
        compiler_params=_params(3),
        name="rglru",
    )(proj, proj, conv_w, conv_b, gate_a_w, gate_a_b, gate_x_w, gate_x_b, lam)


def _qkv_kernel(x_ref, w_ref, c_ref, s1_ref, s2_ref, o_ref, *, tn, d_model, scale):
    col0 = pl.program_id(0) * tn
    is_v = col0 >= 2 * d_model
    qscale = jnp.where(col0 < d_model, scale, 1.0)
    xb = x_ref[...].astype(BF16)
    half = ROPE_DIM // 2
    sub = 2 * ATTN_HEAD_DIM
    c = jnp.concatenate([c_ref[...]] * 2, axis=1)
    s1 = jnp.concatenate([s1_ref[...]] * 2, axis=1)
    s2 = jnp.concatenate([s2_ref[...]] * 2, axis=1)
    for t in range(tn // sub):
        cols = slice(t * sub, (t + 1) * sub)
        acc = _dot(xb, w_ref[:, cols])
        roped = acc * c + pltpu.roll(acc, half, axis=1) * s1 + pltpu.roll(acc, sub - half, axis=1) * s2
        o_ref[:, cols] = (jnp.where(is_v, acc, roped) * qscale).astype(o_ref.dtype)


def _rope_tables(seqlen):
    half = ROPE_DIM // 2
    pos = jnp.arange(seqlen, dtype=F32)
    inv_freq = ROPE_THETA ** (-jnp.arange(0, ROPE_DIM, 2, dtype=F32) / ROPE_DIM)
    ang = pos[:, None] * inv_freq[None, :]
    cos, sin = jnp.cos(ang), jnp.sin(ang)
    pad = ATTN_HEAD_DIM - ROPE_DIM
    c = jnp.concatenate([cos, cos, jnp.ones((seqlen, pad), F32)], axis=1)
    s1 = jnp.concatenate([jnp.zeros((seqlen, half), F32), sin, jnp.zeros((seqlen, pad), F32)], axis=1)
    s2 = jnp.concatenate([-sin, jnp.zeros((seqlen, half + pad), F32)], axis=1)
    return c, s1, s2


def _qkv_rope(x, w_qkv, seqlen, tm=512, tn=512):
    n, d = x.shape
    nout = w_qkv.shape[1]
    tm = min(tm, seqlen)
    tn = min(tn, d)
    c, s1, s2 = _rope_tables(seqlen)
    nt = seqlen // tm
    tab_spec = pl.BlockSpec((tm, ATTN_HEAD_DIM), lambda j, i: (i % nt, 0))
    return pl.pallas_call(
        functools.partial(_qkv_kernel, tn=tn, d_model=d, scale=ATTN_HEAD_DIM**-0.5),
        grid=(nout // tn, n // tm),
        in_specs=[pl.BlockSpec((tm, d), lambda j, i: (i, 0)), pl.BlockSpec((d, tn), lambda j, i: (0, j)), tab_spec, tab_spec, tab_spec],
        out_specs=pl.BlockSpec((tm, tn), lambda j, i: (i, j)),
        out_shape=jax.ShapeDtypeStruct((n, nout), BF16),
        compiler_params=_params(2),
        name="qkv_rope",
    )(x, w_qkv, c, s1, s2)


def _moba_kernel(q_ref, k_ref, v_ref, o_ref, kmean_ref, vt_ref, sel_ref, s_ref, *, nblk, nh):
    i = pl.program_id(2)
    blk, hd = MOBA_BLOCK, ATTN_HEAD_DIM

    heads = range(nh)
    lanes = [slice(h * hd, (h + 1) * hd) for h in heads]

    @pl.when(i == 0)
    def _():
        for h in heads:
            kf = k_ref[:, lanes[h]].astype(F32).reshape(nblk, blk, hd)
            kmean_ref[h] = jnp.sum(kf, axis=1) * (1.0 / blk)
            vt_ref[h] = v_ref[:, lanes[h]].astype(F32).T.astype(BF16)

    q = [q_ref[:, lanes[h]] for h in heads]

    def scores(jp, h):
        off = pl.multiple_of(jp * (2 * blk), 2 * blk)
        return _dot_nt(k_ref[pl.ds(off, 2 * blk), lanes[h]], q[h])

    for h in heads:
        s_ref[0, h] = scores(0, h)

    for h in heads:
        gate = _dot_nt(kmean_ref[h], q[h].astype(F32), precision=HIGHEST)
        blk_id = lax.broadcasted_iota(jnp.int32, gate.shape, 0)
        past = jnp.where(blk_id < i, 1.0, 0.0)
        rows = []
        for n in range(nblk):
            gn = gate[n : n + 1, :]
            tie = jnp.where(blk_id < n, 1.0, 0.0)
            beats = jnp.where(gate > gn, 1.0, jnp.where(gate == gn, tie, 0.0)) * past
            cnt = jnp.sum(beats, axis=0, keepdims=True)
            rows.append(jnp.where(cnt < MOBA_TOPK, 1.0, 0.0) * jnp.where(n < i, 1.0, 0.0))
        sel_ref[h] = jnp.concatenate(rows, axis=0)

    off_own = pl.multiple_of(i * blk, blk)
    key_pos = lax.broadcasted_iota(jnp.int32, (blk, blk), 0)
    qry_pos = lax.broadcasted_iota(jnp.int32, (blk, blk), 1)
    carry = []
    for h in heads:
        s = _dot_nt(k_ref[pl.ds(off_own, blk), lanes[h]], q[h])
        s = jnp.where(key_pos <= qry_pos, s, NEG_INF)
        m = jnp.max(s, axis=0, keepdims=True)
        p = jnp.exp(s - m)
        l = jnp.sum(p, axis=0, keepdims=True)
        acc = _dot(vt_ref[h, :, pl.ds(off_own, blk)], p.astype(BF16))
        carry.append((m, l, acc))

    def body(jp, carry):
        off = pl.multiple_of(jp * (2 * blk), 2 * blk)
        s_cur = [s_ref[jp % 2, h] for h in heads]
        jn = jnp.minimum(jp + 1, nblk // 2 - 1)
        for h in heads:
            s_ref[(jp + 1) % 2, h] = scores(jn, h)
        out = []
        for h in heads:
            m, l, acc = carry[h]
            s0 = jnp.where(sel_ref[h, pl.ds(2 * jp, 1), :] > 0.5, s_cur[h][:blk], NEG_INF)
            s1 = jnp.where(sel_ref[h, pl.ds(2 * jp + 1, 1), :] > 0.5, s_cur[h][blk:], NEG_INF)
            m_blk = jnp.maximum(jnp.max(s0, axis=0, keepdims=True), jnp.max(s1, axis=0, keepdims=True))
            m_new = jnp.maximum(m, m_blk)
            alpha = jnp.exp(m - m_new)
            p0 = jnp.exp(s0 - m_new)
            p1 = jnp.exp(s1 - m_new)
            l = alpha * l + jnp.sum(p0, axis=0, keepdims=True) + jnp.sum(p1, axis=0, keepdims=True)
            p = jnp.concatenate([p0.astype(BF16), p1.astype(BF16)], axis=0)
            acc = acc * alpha + _dot(vt_ref[h, :, pl.ds(off, 2 * blk)], p)
            out.append((m_new, l, acc))
        return tuple(out)

    carry = lax.fori_loop(0, (i + 1) // 2, body, tuple(carry))
    for h in heads:
        m, l, acc = carry[h]
        o_ref[:, lanes[h]] = (acc / l).T.astype(o_ref.dtype)


def _moba_attention(qkv, bsz, seqlen, nh=2):
    n, d3 = qkv.shape
    d = d3 // 3
    ngrp = d // (nh * ATTN_HEAD_DIM)
    nblk = seqlen // MOBA_BLOCK
    assert nblk % 2 == 0
    blk, hd = MOBA_BLOCK, ATTN_HEAD_DIM
    wl = nh * hd
    return pl.pallas_call(
        functools.partial(_moba_kernel, nblk=nblk, nh=nh),
        grid=(bsz, ngrp, nblk),
        in_specs=[
            pl.BlockSpec((blk, wl), lambda b, h, i: (b * nblk + i, h)),
            pl.BlockSpec((seqlen, wl), lambda b, h, i: (b, ngrp + h)),
            pl.BlockSpec((seqlen, wl), lambda b, h, i: (b, 2 * ngrp + h)),
        ],
        out_specs=pl.BlockSpec((blk, wl), lambda b, h, i: (b * nblk + i, h)),
        out_shape=jax.ShapeDtypeStruct((n, d), BF16),
        scratch_shapes=[pltpu.VMEM((nh, nblk, hd), F32), pltpu.VMEM((nh, hd, seqlen), BF16), pltpu.VMEM((nh, nblk, blk), F32),
                        pltpu.VMEM((2, nh, 2 * blk, blk), F32)],
        compiler_params=_params(3),
        name="moba_attention",
    )(qkv, qkv, qkv)


def _token_shift_delta(x, prev_ref, is_start):
    prev_row = jnp.where(is_start, 0.0, prev_ref[SUBLANES - 1 : SUBLANES, :])
    row = lax.broadcasted_iota(jnp.int32, x.shape, 0)
    return jnp.where(row == 0, prev_row, pltpu.roll(x, 1, axis=0)) - x


def _rwkv_proj_kernel(x_ref, prev_ref, mu_ref, w_ref, o_ref, *, tm, seqlen):
    x = x_ref[...]
    xx = _token_shift_delta(x, prev_ref, (pl.program_id(1) * tm) % seqlen == 0)
    o_ref[0] = _dot((x + xx * mu_ref[0]).astype(BF16), w_ref[0])


def _rwkv_proj(x, mu, w_rkv, seqlen, tm=512):
    n, d = x.shape
    tm = min(tm, seqlen)
    step = tm // SUBLANES
    return pl.pallas_call(
        functools.partial(_rwkv_proj_kernel, tm=tm, seqlen=seqlen),
        grid=(3, n // tm),
        in_specs=[
            pl.BlockSpec((tm, d), lambda g, i: (i, 0)),
            pl.BlockSpec((SUBLANES, d), lambda g, i: (jnp.maximum(i * step - 1, 0), 0)),
            pl.BlockSpec((1, 1, d), lambda g, i: (g, 0, 0)),
            pl.BlockSpec((1, d, d), lambda g, i: (g, 0, 0)),
        ],
        out_specs=pl.BlockSpec((1, tm, d), lambda g, i: (g, i, 0)),
        out_shape=jax.ShapeDtypeStruct((3, n, d), F32),
        compiler_params=_params(2),
        name="rwkv_proj",
    )(x, x, mu.reshape(mu.shape[0], 1, d), w_rkv)


def _rwkv_lora_kernel(x_ref, prev_ref, mu_ref, ww1_ref, ww2_ref, aw1_ref, aw2_ref, gw1_ref, gw2_ref, wo_ref, ao_ref, go_ref, *, tm, seqlen):
    x = x_ref[...]
    xx = _token_shift_delta(x, prev_ref, (pl.program_id(0) * tm) % seqlen == 0)
    xw = (x + xx * mu_ref[3:4, :]).astype(BF16)
    xa = (x + xx * mu_ref[4:5, :]).astype(BF16)
    xg = (x + xx * mu_ref[5:6, :]).astype(BF16)
    wo_ref[...] = _dot(jnp.tanh(_dot(xw, ww1_ref[...])).astype(BF16), ww2_ref[...])
    ao_ref[...] = _dot(_dot(xa, aw1_ref[...]).astype(BF16), aw2_ref[...])
    go_ref[...] = _dot(jax.nn.sigmoid(_dot(xg, gw1_ref[...])).astype(BF16), gw2_ref[...])


def _pad_lora(w1, w2):
    rank = w1.shape[1]
    pad = (-rank) % LANES
    return jnp.pad(w1, ((0, 0), (0, pad))), jnp.pad(w2, ((0, pad), (0, 0)))


def _rwkv_lora(x, mu, w_w1, w_w2, a_w1, a_w2, g_w1, g_w2, seqlen, tm=256):
    n, d = x.shape
    tm = min(tm, seqlen)
    step = tm // SUBLANES
    ws = [*_pad_lora(w_w1, w_w2), *_pad_lora(a_w1, a_w2), *_pad_lora(g_w1, g_w2)]
    full = lambda a: pl.BlockSpec(a.shape, lambda i: (0, 0))
    out_spec = pl.BlockSpec((tm, d), lambda i: (i, 0))
    return pl.pallas_call(
        functools.partial(_rwkv_lora_kernel, tm=tm, seqlen=seqlen),
        grid=(n // tm,),
        in_specs=[
            pl.BlockSpec((tm, d), lambda i: (i, 0)),
            pl.BlockSpec((SUBLANES, d), lambda i: (jnp.maximum(i * step - 1, 0), 0)),
            full(mu),
            *[full(w) for w in ws],
        ],
        out_specs=[out_spec, out_spec, out_spec],
        out_shape=[jax.ShapeDtypeStruct((n, d), F32)] * 3,
        compiler_params=_params(1),
        name="rwkv_lora",
    )(x, x, mu, *ws)


def _rwkv_core_kernel(r_ref, k_ref, v_ref, wpre_ref, apre_ref, g_ref, w0_ref, a0_ref, kk_ref, ka_ref, rk_ref, lg_ref, lb_ref,
                      o_ref, st_ref, *, tb, npp):
    c = RWKV_CHUNK
    hd = RWKV_HEAD_DIM

    @pl.when(pl.program_id(2) == 0)
    def _():
        st_ref[...] = jnp.zeros_like(st_ref)

    lane = lax.broadcasted_iota(jnp.int32, (1, RWKV_PAIR), 1)
    m0 = jnp.where(lane < hd, 1.0, 0.0)
    m1 = 1.0 - m0
    ri = lax.broadcasted_iota(jnp.int32, (RWKV_PAIR, RWKV_PAIR), 0)
    ci = lax.broadcasted_iota(jnp.int32, (RWKV_PAIR, RWKV_PAIR), 1)
    same = (ri < hd) == (ci < hd)
    ones_bd = jnp.where(same, 1.0, 0.0)
    strict = jnp.where(same & (ci < ri), 1.0, 0.0)
    incl = jnp.where(same & (ci <= ri), 1.0, 0.0)
    eye = jnp.where(ri == ci, 1.0, 0.0)

    def head_sum(z):
        hi, lo = _split_bf16(z)
        return _dot(jnp.concatenate([hi, lo], axis=1), ones2_b16)

    def stack(z):
        return jnp.concatenate([z * m0, z * m1], axis=0)

    def b16(z):
        return z.astype(BF16)

    ones2_b16 = b16(jnp.concatenate([ones_bd, ones_bd], axis=0))
    zeros_sq = jnp.zeros((RWKV_PAIR, RWKV_PAIR), F32)

    n2 = RWKV_PAIR
    nch = tb // c
    rowc = lax.broadcasted_iota(jnp.int32, (tb, n2), 0) % c

    prep = []
    for p in range(npp):
        ln = slice(p * n2, (p + 1) * n2)
        r = r_ref[0, :, ln]
        k = k_ref[0, :, ln]
        v = v_ref[0, :, ln]
        w_log = -_softplus(-(w0_ref[:, ln] + wpre_ref[:, ln])) - 0.5
        lw = -jnp.exp(w_log)
        a = jax.nn.sigmoid(a0_ref[:, ln] + apre_ref[:, ln])
        kk = k * kk_ref[:, ln]
        kk = kk / jnp.maximum(jnp.sqrt(head_sum(kk * kk)), 1e-12)
        k2 = k * (1.0 + (a - 1.0) * ka_ref[:, ln])
        bv = kk * a
        cl = lw
        d = 1
        while d < c:
            cl = cl + jnp.where(rowc >= d, pltpu.roll(cl, d, axis=0), 0.0)
            d *= 2
        prep.append((r, k2, v, kk, bv, lw, cl))

    items = [(p, ch) for p in range(npp) for ch in range(nch)]
    rt, at16, v16, lhs16, diag, a_ab, a_ak16 = {}, {}, {}, {}, {}, {}, {}
    for it in items:
        p, ch = it
        r, k2, v, kk, bv, lw, cl = prep[p]
        sl = slice(ch * c, (ch + 1) * c)
        clc = cl[sl]
        cle = clc[c - 1 : c, :]
        e_neg = jnp.exp(-clc)
        e_end = jnp.exp(cle - clc)
        rt_s = stack(r[sl] * jnp.exp(clc))
        at_s = stack(-kk[sl] * jnp.exp(clc - lw[sl]))
        gmat = _dot_nt(b16(jnp.concatenate([at_s, rt_s], axis=0)),
                       b16(jnp.concatenate([stack(bv[sl] * e_neg), stack(k2[sl] * e_neg)], axis=0)))
        a_ab[it] = gmat[:n2, :n2] * strict
        a_ak16[it] = b16(gmat[:n2, n2:] * strict)
        ends_t = jnp.concatenate([stack(bv[sl] * e_end).T, stack(k2[sl] * e_end).T], axis=1)
        lhs16[it] = b16(jnp.concatenate([gmat[n2:] * jnp.concatenate([incl, incl], axis=1), ends_t], axis=0))
        rt[it] = rt_s
        at16[it] = b16(at_s)
        v16[it] = b16(stack(v[sl]))
        diag[it] = eye * jnp.exp(cle)

    av16 = {it: b16(_dot(a_ak16[it], v16[it])) for it in items}
    tinv = {it: eye + a_ab[it] for it in items}
    pw = {it: _dot(b16(a_ab[it]), b16(a_ab[it])) for it in items}
    span = 2
    while span < c:
        span *= 2
        for it in items:
            if span < c:
                res = _dot(b16(pw[it]), b16(jnp.concatenate([tinv[it], pw[it]], axis=1)))
                tinv[it] = tinv[it] + res[:, :n2]
                pw[it] = res[:, n2:]
            else:
                tinv[it] = tinv[it] + _dot(b16(pw[it]), b16(tinv[it]))

    big = {}
    for it in items:
        wu16 = b16(_dot(b16(tinv[it]), jnp.concatenate([at16[it], av16[it]], axis=1)))
        rhs16 = jnp.concatenate([wu16, jnp.concatenate([b16(zeros_sq), v16[it]], axis=1)], axis=0)
        big[it] = _dot(lhs16[it], rhs16)

    st = {p: st_ref[p] for p in range(npp)}
    st_in = {}
    for ch in range(nch):
        for p in range(npp):
            it = (p, ch)
            st_in[it] = st[p]
            st[p] = _dot_bf16x3(diag[it] + big[it][n2:, :n2], st[p]) + big[it][n2:, n2:]
    for p in range(npp):
        st_ref[p] = st[p]

    for p in range(npp):
        ln = slice(p * n2, (p + 1) * n2)
        r, k2, v, kk, bv, lw, cl = prep[p]
        ys = []
        for ch in range(nch):
            it = (p, ch)
            y_s = _dot(b16(rt[it] + big[it][:n2, :n2]), b16(st_in[it])) + big[it][:n2, n2:]
            ys.append(y_s[:c] + y_s[c:])
        y = jnp.concatenate(ys, axis=0)
        mean = head_sum(y) * (1.0 / hd)
        yc = y - mean
        var = head_sum(yc * yc) * (1.0 / hd)
        yn = yc * lax.rsqrt(var + RWKV_LNX_EPS) * lg_ref[:, ln] + lb_ref[:, ln]
        bonus = head_sum(r * k2 * rk_ref[:, ln]) * v
        o_ref[:, ln] = ((yn + bonus) * g_ref[:, ln]).astype(o_ref.dtype)


def _rwkv_core(rkv, w_pre, a_pre, g, w0, a0, k_k, k_a, r_k, lnx_g, lnx_b, bsz, seqlen, tb=512, npp=2):
    _, n, d = rkv.shape
    tb = min(tb, seqlen)
    nt = seqlen // tb
    wl = npp * RWKV_PAIR
    rkv_spec = lambda j: pl.BlockSpec((1, tb, wl), lambda b, p, t: (j, b * nt + t, p))
    act_spec = pl.BlockSpec((tb, wl), lambda b, p, t: (b * nt + t, p))
    row_spec = pl.BlockSpec((1, wl), lambda b, p, t: (0, p))
    return pl.pallas_call(
        functools.partial(_rwkv_core_kernel, tb=tb, npp=npp),
        grid=(bsz, d // wl, nt),
        in_specs=[rkv_spec(0), rkv_spec(1), rkv_spec(2), act_spec, act_spec, act_spec] + [row_spec] * 7,
        out_specs=act_spec,
        out_shape=jax.ShapeDtypeStruct((n, d), BF16),
        scratch_shapes=[pltpu.VMEM((npp, RWKV_PAIR, RWKV_PAIR), F32)],
        compiler_params=_params(3),
        name="rwkv_core",
    )(rkv, rkv, rkv, w_pre, a_pre, g, w0, a0, k_k, k_a, r_k, lnx_g, lnx_b)


def _pool_kernel(x_ref, prev_ref, w_ref, sc_ref, g_ref, b_ref, o_ref, *, tm, seqlen, alpha):
    start = (pl.program_id(0) * tm) % seqlen
    x = x_ref[...]
    prev = jnp.where(start == 0, 0.0, prev_ref[...])
    xe = jnp.concatenate([prev, x], axis=0)
    pos = start + lax.broadcasted_iota(jnp.int32, (tm, 1), 0)
    gw = w_ref.shape[1]
    ys = []
    for gi, win in enumerate(POOL_WINDOWS):
        sl = slice(gi * gw, (gi + 1) * gw)
        s = xe[:, sl]
        d = 1
        while d < win:
            s = s + pltpu.roll(s, d, axis=0)
            d *= 2
        cnt = jnp.minimum(pos + 1, win).astype(F32)
        pooled = s[POOL_HALO:] / cnt - x[:, sl]
        ys.append(_dot(pooled.astype(BF16), w_ref[gi]))
    y = jnp.concatenate(ys, axis=1) * sc_ref[...]
    o_ref[...] = _layer_norm_rows(alpha * x + y, g_ref[...], b_ref[...])


def _pool_ln(x, w_pool, scale, g, b, seqlen, alpha, tm=512):
    n, d = x.shape
    tm = min(tm, seqlen)
    step = tm // POOL_HALO
    row_spec = pl.BlockSpec((1, d), lambda i: (0, 0))
    return pl.pallas_call(
        functools.partial(_pool_kernel, tm=tm, seqlen=seqlen, alpha=alpha),
        grid=(n // tm,),
        in_specs=[
            pl.BlockSpec((tm, d), lambda i: (i, 0)),
            pl.BlockSpec((POOL_HALO, d), lambda i: (jnp.maximum(i * step - 1, 0), 0)),
            pl.BlockSpec(w_pool.shape, lambda i: (0, 0, 0)),
            row_spec,
            row_spec,
            row_spec,
        ],
        out_specs=pl.BlockSpec((tm, d), lambda i: (i, 0)),
        out_shape=jax.ShapeDtypeStruct((n, d), F32),
        compiler_params=_params(1),
        name="pool_ln",
    )(x, x, w_pool, scale, g, b)


def kernel(x, ln_g, ln_b, mlp_w1, mlp_w2, rg_w_in, rg_conv_w, rg_conv_b, rg_gate_a_w, rg_gate_a_b, rg_gate_x_w, rg_gate_x_b, rg_lambda, rg_w_out, moba_w_qkv, moba_w_out, rwkv_mu, rwkv_w_rkv, rwkv_w0, rwkv_w_w1, rwkv_w_w2, rwkv_a0, rwkv_a_w1, rwkv_a_w2, rwkv_g_w1, rwkv_g_w2, rwkv_k_k, rwkv_k_a, rwkv_r_k, rwkv_lnx_g, rwkv_lnx_b, rwkv_w_out, pool_w, pool_scale):
    bsz, seqlen, d = x.shape
    depth = ln_g.shape[0]
    n_mixers = 4
    alpha = (2.0 * depth) ** 0.25
    bf = lambda w: w.astype(BF16)
    row = lambda p: p.reshape(1, -1)

    h = x.reshape(bsz * seqlen, d)
    counts = [0] * n_mixers
    for layer in range(depth):
        m = layer % n_mixers
        j = counts[m]
        counts[m] += 1
        g0, b0 = row(ln_g[layer, 0]), row(ln_b[layer, 0])
        if m == 0:
            proj = _matmul(h, bf(rg_w_in[j]), F32)
            mixed = _rglru(proj, rg_conv_w[j], row(rg_conv_b[j]), bf(rg_gate_a_w[j]), row(rg_gate_a_b[j]),
                           bf(rg_gate_x_w[j]), row(rg_gate_x_b[j]), row(rg_lambda[j]), bsz, seqlen)
            h = _matmul_ln(mixed, bf(rg_w_out[j]), h, g0, b0, alpha)
        elif m == 1:
            qkv = _qkv_rope(h, bf(moba_w_qkv[j]), seqlen)
            att = _moba_attention(qkv, bsz, seqlen)
            h = _matmul_ln(att, bf(moba_w_out[j]), h, g0, b0, alpha)
        elif m == 2:
            rkv = _rwkv_proj(h, rwkv_mu[j], bf(rwkv_w_rkv[j]), seqlen)
            w_pre, a_pre, gate = _rwkv_lora(h, rwkv_mu[j], bf(rwkv_w_w1[j]), bf(rwkv_w_w2[j]), bf(rwkv_a_w1[j]),
                                            bf(rwkv_a_w2[j]), bf(rwkv_g_w1[j]), bf(rwkv_g_w2[j]), seqlen)
            mixed = _rwkv_core(rkv, w_pre, a_pre, gate, row(rwkv_w0[j]), row(rwkv_a0[j]), row(rwkv_k_k[j]), row(rwkv_k_a[j]),
                               row(rwkv_r_k[j]), row(rwkv_lnx_g[j]), row(rwkv_lnx_b[j]), bsz, seqlen)
            h = _matmul_ln(mixed, bf(rwkv_w_out[j]), h, g0, b0, alpha)
        else:
            h = _pool_ln(h, bf(pool_w[j]), row(pool_scale[j]), g0, b0, seqlen, alpha)
        h = _mlp_ln(h, bf(mlp_w1[layer]), bf(mlp_w2[layer]), row(ln_g[layer, 1]), row(ln_b[layer, 1]), alpha)
    return h.reshape(bsz, seqlen, d)
```

```python
import functools

import jax
import jax.numpy as jnp
from jax import lax
from jax.experimental import pallas as pl
from jax.experimental.pallas import tpu as pltpu

F32 = jnp.float32
BF16 = jnp.bfloat16
HIGHEST = lax.Precision.HIGHEST

LN_EPS = 1e-5
NEG_INF = -1e30
VMEM_LIMIT_BYTES = 48 * 1024 * 1024

LRU_C = 8.0
CONV_WIDTH = 4
ATTN_HEAD_DIM = 128
MOBA_BLOCK = 256
MOBA_TOPK = 3
ROPE_THETA = 500000.0
ROPE_DIM = ATTN_HEAD_DIM // 4
RWKV_HEAD_DIM = 64
RWKV_PAIR = 2 * RWKV_HEAD_DIM
RWKV_CHUNK = 64
RWKV_LNX_EPS = 64e-5
POOL_WINDOWS = (2, 4, 8, 16)
POOL_HALO = 16
LANES = 128
SUBLANES = 8


def _params(n_axes):
    return pltpu.CompilerParams(dimension_semantics=("arbitrary",) * n_axes, vmem_limit_bytes=VMEM_LIMIT_BYTES)


def _layer_norm_rows(z, g, b):
    mean = jnp.mean(z, axis=-1, keepdims=True)
    zc = z - mean
    var = jnp.mean(zc * zc, axis=-1, keepdims=True)
    return zc * lax.rsqrt(var + LN_EPS) * g + b


def _softplus(z):
    return jnp.maximum(z, 0.0) + jnp.log1p(jnp.exp(-jnp.abs(z)))


def _dot(a, b, precision=None):
    return jnp.dot(a, b, preferred_element_type=F32, precision=precision)


def _split_bf16(z):
    hi = z.astype(BF16)
    return hi, (z - hi.astype(F32)).astype(BF16)


def _dot_bf16x3(a, b):
    a_hi, a_lo = _split_bf16(a)
    b_hi, b_lo = _split_bf16(b)
    return _dot(jnp.concatenate([a_hi, a_lo], axis=1), jnp.concatenate([b_hi, b_hi], axis=0)) + _dot(a_hi, b_lo)


def _dot_nt(a, b, precision=None):
    return lax.dot_general(a, b, (((1,), (1,)), ((), ())), preferred_element_type=F32, precision=precision)


def _mm_kernel(a_ref, w_ref, o_ref):
    o_ref[...] = _dot(a_ref[...].astype(BF16), w_ref[...]).astype(o_ref.dtype)


def _matmul(a, w, out_dtype, tm=512, tn=1024):
    n, k = a.shape
    nout = w.shape[1]
    tn = min(tn, nout)
    return pl.pallas_call(
        _mm_kernel,
        grid=(nout // tn, n // tm),
        in_specs=[pl.BlockSpec((tm, k), lambda j, i: (i, 0)), pl.BlockSpec((k, tn), lambda j, i: (0, j))],
        out_specs=pl.BlockSpec((tm, tn), lambda j, i: (i, j)),
        out_shape=jax.ShapeDtypeStruct((n, nout), out_dtype),
        compiler_params=_params(2),
        name="matmul",
    )(a, w)


def _mm_ln_kernel(a_ref, w_ref, x_ref, g_ref, b_ref, o_ref, acc_ref, *, alpha):
    @pl.when(pl.program_id(0) == 0)
    def _():
        acc_ref[...] = jnp.zeros_like(acc_ref)

    o_ref[...] = _layer_norm_rows(alpha * x_ref[...] + acc_ref[...], g_ref[...], b_ref[...])
    acc_ref[...] = _dot(a_ref[...].astype(BF16), w_ref[...])


def _matmul_ln(a, w, x, g, b, alpha, tm=512):
    n, k = a.shape
    d = w.shape[1]
    ni = n // tm
    cur = lambda s: jnp.minimum(s, ni - 1)
    lag = lambda s: jnp.maximum(s - 1, 0)
    return pl.pallas_call(
        functools.partial(_mm_ln_kernel, alpha=alpha),
        grid=(ni + 1,),
        in_specs=[
            pl.BlockSpec((tm, k), lambda s: (cur(s), 0)),
            pl.BlockSpec((k, d), lambda s: (0, 0)),
            pl.BlockSpec((tm, d), lambda s: (lag(s), 0)),
            pl.BlockSpec((1, d), lambda s: (0, 0)),
            pl.BlockSpec((1, d), lambda s: (0, 0)),
        ],
        out_specs=pl.BlockSpec((tm, d), lambda s: (lag(s), 0)),
        out_shape=jax.ShapeDtypeStruct((n, d), F32),
        scratch_shapes=[pltpu.VMEM((tm, d), F32)],
        compiler_params=_params(1),
        name="matmul_ln",
    )(a, w, x, g, b)


def _mlp_kernel(x_ref, w1_ref, w2_ref, g_ref, b_ref, o_ref, xb_ref, acc_ref, *, alpha):
    f = pl.program_id(1)

    @pl.when(f == 0)
    def _():
        xb_ref[...] = x_ref[...].astype(BF16)
        acc_ref[...] = jnp.zeros_like(acc_ref)

    h = jnp.maximum(_dot(xb_ref[...], w1_ref[...]), 0.0)
    acc_ref[...] += _dot((h * h).astype(BF16), w2_ref[...])

    @pl.when(f == pl.num_programs(1) - 1)
    def _():
        o_ref[...] = _layer_norm_rows(alpha * x_ref[...] + acc_ref[...], g_ref[...], b_ref[...])


def _mlp_ln(x, w1, w2, g, b, alpha, tm=512, tf=1024):
    n, d = x.shape
    dff = w1.shape[1]
    return pl.pallas_call(
        functools.partial(_mlp_kernel, alpha=alpha),
        grid=(n // tm, dff // tf),
        in_specs=[
            pl.BlockSpec((tm, d), lambda i, f: (i, 0)),
            pl.BlockSpec((d, tf), lambda i, f: (0, f)),
            pl.BlockSpec((tf, d), lambda i, f: (f, 0)),
            pl.BlockSpec((1, d), lambda i, f: (0, 0)),
            pl.BlockSpec((1, d), lambda i, f: (0, 0)),
        ],
        out_specs=pl.BlockSpec((tm, d), lambda i, f: (i, 0)),
        out_shape=jax.ShapeDtypeStruct((n, d), F32),
        scratch_shapes=[pltpu.VMEM((tm, d), BF16), pltpu.VMEM((tm, d), F32)],
        compiler_params=_params(2),
        name="mlp_ln",
    )(x, w1, w2, g, b)


def _rglru_kernel(gate_ref, u_ref, cw_ref, cb_ref, aw_ref, ab_ref, xw_ref, xb_ref, lam_ref, o_ref, hc_ref, ut_ref, *, ts):
    @pl.when(pl.program_id(2) == 0)
    def _():
        hc_ref[...] = jnp.zeros_like(hc_ref)
        ut_ref[...] = jnp.zeros_like(ut_ref)

    u = u_ref[...]
    ue = jnp.concatenate([ut_ref[...], u], axis=0)
    conv = cw_ref[CONV_WIDTH - 1 : CONV_WIDTH, :] * u + cb_ref[...]
    for d in range(1, CONV_WIDTH):
        conv = conv + cw_ref[CONV_WIDTH - 1 - d : CONV_WIDTH - d, :] * pltpu.roll(ue, d, axis=0)[SUBLANES:]
    ut_ref[...] = u[ts - SUBLANES :, :]

    cb16 = conv.astype(BF16)
    r = jax.nn.sigmoid(_dot(cb16, aw_ref[0]) + ab_ref[...])
    i = jax.nn.sigmoid(_dot(cb16, xw_ref[0]) + xb_ref[...])
    log_a = (-LRU_C) * r * _softplus(-lam_ref[...])
    a = jnp.exp(log_a)
    b = conv * i * jnp.sqrt(1.0 - a * a)

    w = a.shape[1]
    a = a.reshape(ts // SUBLANES, SUBLANES, w)
    b = b.reshape(ts // SUBLANES, SUBLANES, w)
    row = lax.broadcasted_iota(jnp.int32, a.shape, 1)
    d = 1
    while d < SUBLANES:
        keep = row >= d
        a_sh = jnp.where(keep, pltpu.roll(a, d, axis=1), 1.0)
        b_sh = jnp.where(keep, pltpu.roll(b, d, axis=1), 0.0)
        b = a * b_sh + b
        a = a * a_sh
        d *= 2
    h_prev = hc_ref[0:1, :]
    hs = []
    for g in range(ts // SUBLANES):
        hg = a[g] * h_prev + b[g]
        hs.append(hg)
        h_prev = hg[SUBLANES - 1 : SUBLANES, :]
    h = jnp.concatenate(hs, axis=0)
    hc_ref[...] = jnp.broadcast_to(h_prev, hc_ref.shape)
    o_ref[...] = (jax.nn.gelu(gate_ref[...]) * h).astype(o_ref.dtype)


def _rglru(proj, conv_w, conv_b, gate_a_w, gate_a_b, gate_x_w, gate_x_b, lam, bsz, seqlen, ts=512):
    n, r2 = proj.shape
    r = r2 // 2
    nblk, w, _ = gate_a_w.shape
    ts = min(ts, seqlen)
    nt = seqlen // ts
    row_spec = pl.BlockSpec((1, w), lambda b, j, t: (0, j))
    gw_spec = pl.BlockSpec((1, w, w), lambda b, j, t: (j, 0, 0))
    return pl.pallas_call(
        functools.partial(_rglru_kernel, ts=ts),
        grid=(bsz, nblk, nt),
        in_specs=[
            pl.BlockSpec((ts, w), lambda b, j, t: (b * nt + t, j)),
            pl.BlockSpec((ts, w), lambda b, j, t: (b * nt + t, nblk + j)),
            pl.BlockSpec((CONV_WIDTH, w), lambda b, j, t: (0, j)),
            row_spec,
            gw_spec,
            row_spec,
            gw_spec,
            row_spec,
            row_spec,
        ],
        out_specs=pl.BlockSpec((ts, w), lambda b, j, t: (b * nt + t, j)),
        out_shape=jax.ShapeDtypeStruct((n, r), BF16),
        scratch_shapes=[pltpu.VMEM((SUBLANES, w), F32), pltpu.VMEM((SUBLANES, w), F32)],
        compiler_params=_params(3),
        name="rglru",
    )(proj, proj, conv_w, conv_b, gate_a_w, gate_a_b, gate_x_w, gate_x_b, lam)


def _qkv_kernel(x_ref, w_ref, c_ref, s1_ref, s2_ref, o_ref, acc_ref, *, tn):
    @pl.when(pl.program_id(0) == 0)
    def _():
        acc_ref[...] = jnp.zeros_like(acc_ref)

    half = ROPE_DIM // 2
    rep = tn // ATTN_HEAD_DIM
    prev = acc_ref[...]
    c = jnp.concatenate([c_ref[0]] * rep, axis=1)
    s1 = jnp.concatenate([s1_ref[0]] * rep, axis=1)
    s2 = jnp.concatenate([s2_ref[0]] * rep, axis=1)
    roped = prev * c + pltpu.roll(prev, half, axis=1) * s1 + pltpu.roll(prev, tn - half, axis=1) * s2
    o_ref[...] = roped.astype(o_ref.dtype)
    acc_ref[...] = _dot(x_ref[...].astype(BF16), w_ref[...])


def _rope_tables(seqlen, scale):
    half = ROPE_DIM // 2
    pos = jnp.arange(seqlen, dtype=F32)
    inv_freq = ROPE_THETA ** (-jnp.arange(0, ROPE_DIM, 2, dtype=F32) / ROPE_DIM)
    ang = pos[:, None] * inv_freq[None, :]
    cos, sin = jnp.cos(ang), jnp.sin(ang)
    pad = ATTN_HEAD_DIM - ROPE_DIM
    c = jnp.concatenate([cos, cos, jnp.ones((seqlen, pad), F32)], axis=1)
    s1 = jnp.concatenate([jnp.zeros((seqlen, half), F32), sin, jnp.zeros((seqlen, pad), F32)], axis=1)
    s2 = jnp.concatenate([-sin, jnp.zeros((seqlen, half + pad), F32)], axis=1)
    ident = jnp.ones_like(c)
    zero = jnp.zeros_like(c)
    return (jnp.stack([c * scale, c, ident]), jnp.stack([s1 * scale, s1, zero]), jnp.stack([s2 * scale, s2, zero]))


def _qkv_rope(x, w_qkv, seqlen, tm=512, tn=512):
    n, d = x.shape
    nout = w_qkv.shape[1]
    tm = min(tm, seqlen)
    tn = min(tn, d)
    c, s1, s2 = _rope_tables(seqlen, ATTN_HEAD_DIM**-0.5)
    nt = seqlen // tm
    ni, nj = n // tm, nout // tn
    last = ni * nj - 1
    cur = lambda s: jnp.minimum(s, last)
    lag = lambda s: jnp.maximum(s - 1, 0)
    tab_spec = pl.BlockSpec((1, tm, ATTN_HEAD_DIM), lambda s: ((lag(s) // ni) * tn // d, (lag(s) % ni) % nt, 0))
    return pl.pallas_call(
        functools.partial(_qkv_kernel, tn=tn),
        grid=(ni * nj + 1,),
        in_specs=[
            pl.BlockSpec((tm, d), lambda s: (cur(s) % ni, 0)),
            pl.BlockSpec((d, tn), lambda s: (0, cur(s) // ni)),
            tab_spec,
            tab_spec,
            tab_spec,
        ],
        out_specs=pl.BlockSpec((tm, tn), lambda s: (lag(s) % ni, lag(s) // ni)),
        out_shape=jax.ShapeDtypeStruct((n, nout), BF16),
        scratch_shapes=[pltpu.VMEM((tm, tn), F32)],
        compiler_params=_params(1),
        name="qkv_rope",
    )(x, w_qkv, c, s1, s2)


def _moba_kernel(q_ref, k_ref, v_ref, o_ref, kmean_ref, vt_ref, sel_ref, s_ref, *, nblk, nh):
    i = pl.program_id(2)
    blk, hd = MOBA_BLOCK, ATTN_HEAD_DIM

    heads = range(nh)
    lanes = [slice(h * hd, (h + 1) * hd) for h in heads]

    @pl.when(i == 0)
    def _():
        for h in heads:
            kf = k_ref[:, lanes[h]].astype(F32).reshape(nblk, blk, hd)
            kmean_ref[h] = jnp.sum(kf, axis=1) * (1.0 / blk)
            vt_ref[h] = v_ref[:, lanes[h]].astype(F32).T.astype(BF16)

    q = [q_ref[:, lanes[h]] for h in heads]

    def scores(jp, h):
        off = pl.multiple_of(jp * (2 * blk), 2 * blk)
        return _dot_nt(k_ref[pl.ds(off, 2 * blk), lanes[h]], q[h])

    for h in heads:
        s_ref[0, h] = scores(0, h)

    for h in heads:
        gate = _dot_nt(kmean_ref[h], q[h].astype(F32), precision=HIGHEST)
        blk_id = lax.broadcasted_iota(jnp.int32, gate.shape, 0)
        past = jnp.where(blk_id < i, 1.0, 0.0)
        rows = []
        for n in range(nblk):
            gn = gate[n : n + 1, :]
            tie = jnp.where(blk_id < n, 1.0, 0.0)
            beats = jnp.where(gate > gn, 1.0, jnp.where(gate == gn, tie, 0.0)) * past
            cnt = jnp.sum(beats, axis=0, keepdims=True)
            rows.append(jnp.where(cnt < MOBA_TOPK, 1.0, 0.0) * jnp.where(n < i, 1.0, 0.0))
        sel_ref[h] = jnp.concatenate(rows, axis=0)

    off_own = pl.multiple_of(i * blk, blk)
    key_pos = lax.broadcasted_iota(jnp.int32, (blk, blk), 0)
    qry_pos = lax.broadcasted_iota(jnp.int32, (blk, blk), 1)
    carry = []
    for h in heads:
        s = _dot_nt(k_ref[pl.ds(off_own, blk), lanes[h]], q[h])
        s = jnp.where(key_pos <= qry_pos, s, NEG_INF)
        m = jnp.max(s, axis=0, keepdims=True)
        p = jnp.exp(s - m)
        l = jnp.sum(p, axis=0, keepdims=True)
        acc = _dot(vt_ref[h, :, pl.ds(off_own, blk)], p.astype(BF16))
        carry.append((m, l, acc))

    def body(jp, carry):
        off = pl.multiple_of(jp * (2 * blk), 2 * blk)
        s_cur = [s_ref[jp % 2, h] for h in heads]
        jn = jnp.minimum(jp + 1, nblk // 2 - 1)
        for h in heads:
            s_ref[(jp + 1) % 2, h] = scores(jn, h)
        out = []
        for h in heads:
            m, l, acc = carry[h]
            s0 = jnp.where(sel_ref[h, pl.ds(2 * jp, 1), :] > 0.5, s_cur[h][:blk], NEG_INF)
            s1 = jnp.where(sel_ref[h, pl.ds(2 * jp + 1, 1), :] > 0.5, s_cur[h][blk:], NEG_INF)
            m_blk = jnp.maximum(jnp.max(s0, axis=0, keepdims=True), jnp.max(s1, axis=0, keepdims=True))
            m_new = jnp.maximum(m, m_blk)
            alpha = jnp.exp(m - m_new)
            p0 = jnp.exp(s0 - m_new)
            p1 = jnp.exp(s1 - m_new)
            l = alpha * l + jnp.sum(p0, axis=0, keepdims=True) + jnp.sum(p1, axis=0, keepdims=True)
            p = jnp.concatenate([p0.astype(BF16), p1.astype(BF16)], axis=0)
            acc = acc * alpha + _dot(vt_ref[h, :, pl.ds(off, 2 * blk)], p)
            out.append((m_new, l, acc))
        return tuple(out)

    carry = lax.fori_loop(0, (i + 1) // 2, body, tuple(carry))
    for h in heads:
        m, l, acc = carry[h]
        o_ref[:, lanes[h]] = (acc / l).T.astype(o_ref.dtype)


def _moba_attention(qkv, bsz, seqlen, nh=2):
    n, d3 = qkv.shape
    d = d3 // 3
    ngrp = d // (nh * ATTN_HEAD_DIM)
    nblk = seqlen // MOBA_BLOCK
    assert nblk % 2 == 0
    blk, hd = MOBA_BLOCK, ATTN_HEAD_DIM
    wl = nh * hd
    return pl.pallas_call(
        functools.partial(_moba_kernel, nblk=nblk, nh=nh),
        grid=(bsz, ngrp, nblk),
        in_specs=[
            pl.BlockSpec((blk, wl), lambda b, h, i: (b * nblk + i, h)),
            pl.BlockSpec((seqlen, wl), lambda b, h, i: (b, ngrp + h)),
            pl.BlockSpec((seqlen, wl), lambda b, h, i: (b, 2 * ngrp + h)),
        ],
        out_specs=pl.BlockSpec((blk, wl), lambda b, h, i: (b * nblk + i, h)),
        out_shape=jax.ShapeDtypeStruct((n, d), BF16),
        scratch_shapes=[pltpu.VMEM((nh, nblk, hd), F32), pltpu.VMEM((nh, hd, seqlen), BF16), pltpu.VMEM((nh, nblk, blk), F32),
                        pltpu.VMEM((2, nh, 2 * blk, blk), F32)],
        compiler_params=_params(3),
        name="moba_attention",
    )(qkv, qkv, qkv)


def _token_shift_delta(x, prev_ref, is_start):
    prev_row = jnp.where(is_start, 0.0, prev_ref[SUBLANES - 1 : SUBLANES, :])
    row = lax.broadcasted_iota(jnp.int32, x.shape, 0)
    return jnp.where(row == 0, prev_row, pltpu.roll(x, 1, axis=0)) - x


def _rwkv_proj_kernel(x_ref, prev_ref, mu_ref, w_ref, o_ref, *, tm, seqlen):
    x = x_ref[...]
    xx = _token_shift_delta(x, prev_ref, (pl.program_id(1) * tm) % seqlen == 0)
    o_ref[0] = _dot((x + xx * mu_ref[0]).astype(BF16), w_ref[0])


def _rwkv_proj(x, mu, w_rkv, seqlen, tm=512):
    n, d = x.shape
    tm = min(tm, seqlen)
    step = tm // SUBLANES
    return pl.pallas_call(
        functools.partial(_rwkv_proj_kernel, tm=tm, seqlen=seqlen),
        grid=(3, n // tm),
        in_specs=[
            pl.BlockSpec((tm, d), lambda g, i: (i, 0)),
            pl.BlockSpec((SUBLANES, d), lambda g, i: (jnp.maximum(i * step - 1, 0), 0)),
            pl.BlockSpec((1, 1, d), lambda g, i: (g, 0, 0)),
            pl.BlockSpec((1, d, d), lambda g, i: (g, 0, 0)),
        ],
        out_specs=pl.BlockSpec((1, tm, d), lambda g, i: (g, i, 0)),
        out_shape=jax.ShapeDtypeStruct((3, n, d), F32),
        compiler_params=_params(2),
        name="rwkv_proj",
    )(x, x, mu.reshape(mu.shape[0], 1, d), w_rkv)


def _rwkv_lora_kernel(x_ref, prev_ref, mu_ref, ww1_ref, ww2_ref, aw1_ref, aw2_ref, gw1_ref, gw2_ref, wo_ref, ao_ref, go_ref, *, tm, seqlen):
    x = x_ref[...]
    xx = _token_shift_delta(x, prev_ref, (pl.program_id(0) * tm) % seqlen == 0)
    xw = (x + xx * mu_ref[3:4, :]).astype(BF16)
    xa = (x + xx * mu_ref[4:5, :]).astype(BF16)
    xg = (x + xx * mu_ref[5:6, :]).astype(BF16)
    wo_ref[...] = _dot(jnp.tanh(_dot(xw, ww1_ref[...])).astype(BF16), ww2_ref[...])
    ao_ref[...] = _dot(_dot(xa, aw1_ref[...]).astype(BF16), aw2_ref[...])
    go_ref[...] = _dot(jax.nn.sigmoid(_dot(xg, gw1_ref[...])).astype(BF16), gw2_ref[...])


def _pad_lora(w1, w2):
    rank = w1.shape[1]
    pad = (-rank) % LANES
    return jnp.pad(w1, ((0, 0), (0, pad))), jnp.pad(w2, ((0, pad), (0, 0)))


def _rwkv_lora(x, mu, w_w1, w_w2, a_w1, a_w2, g_w1, g_w2, seqlen, tm=256):
    n, d = x.shape
    tm = min(tm, seqlen)
    step = tm // SUBLANES
    ws = [*_pad_lora(w_w1, w_w2), *_pad_lora(a_w1, a_w2), *_pad_lora(g_w1, g_w2)]
    full = lambda a: pl.BlockSpec(a.shape, lambda i: (0, 0))
    out_spec = pl.BlockSpec((tm, d), lambda i: (i, 0))
    return pl.pallas_call(
        functools.partial(_rwkv_lora_kernel, tm=tm, seqlen=seqlen),
        grid=(n // tm,),
        in_specs=[
            pl.BlockSpec((tm, d), lambda i: (i, 0)),
            pl.BlockSpec((SUBLANES, d), lambda i: (jnp.maximum(i * step - 1, 0), 0)),
            full(mu),
            *[full(w) for w in ws],
        ],
        out_specs=[out_spec, out_spec, out_spec],
        out_shape=[jax.ShapeDtypeStruct((n, d), F32)] * 3,
        compiler_params=_params(1),
        name="rwkv_lora",
    )(x, x, mu, *ws)


def _rwkv_core_kernel(r_ref, k_ref, v_ref, wpre_ref, apre_ref, g_ref, w0_ref, a0_ref, kk_ref, ka_ref, rk_ref, lg_ref, lb_ref,
                      o_ref, st_ref, *, tb, npp):
    c = RWKV_CHUNK
    hd = RWKV_HEAD_DIM

    @pl.when(pl.program_id(2) == 0)
    def _():
        st_ref[...] = jnp.zeros_like(st_ref)

    lane = lax.broadcasted_iota(jnp.int32, (1, RWKV_PAIR), 1)
    m0 = jnp.where(lane < hd, 1.0, 0.0)
    m1 = 1.0 - m0
    ri = lax.broadcasted_iota(jnp.int32, (RWKV_PAIR, RWKV_PAIR), 0)
    ci = lax.broadcasted_iota(jnp.int32, (RWKV_PAIR, RWKV_PAIR), 1)
    same = (ri < hd) == (ci < hd)
    ones_bd = jnp.where(same, 1.0, 0.0)
    strict = jnp.where(same & (ci < ri), 1.0, 0.0)
    incl = jnp.where(same & (ci <= ri), 1.0, 0.0)
    eye = jnp.where(ri == ci, 1.0, 0.0)

    def head_sum(z):
        hi, lo = _split_bf16(z)
        return _dot(jnp.concatenate([hi, lo], axis=1), ones2_b16)

    def stack(z):
        return jnp.concatenate([z * m0, z * m1], axis=0)

    def b16(z):
        return z.astype(BF16)

    ones2_b16 = b16(jnp.concatenate([ones_bd, ones_bd], axis=0))
    zeros_sq = jnp.zeros((RWKV_PAIR, RWKV_PAIR), F32)

    n2 = RWKV_PAIR
    nch = tb // c
    rowc = lax.broadcasted_iota(jnp.int32, (tb, n2), 0) % c

    prep = []
    for p in range(npp):
        ln = slice(p * n2, (p + 1) * n2)
        r = r_ref[0, :, ln]
        k = k_ref[0, :, ln]
        v = v_ref[0, :, ln]
        w_log = -_softplus(-(w0_ref[:, ln] + wpre_ref[:, ln])) - 0.5
        lw = -jnp.exp(w_log)
        a = jax.nn.sigmoid(a0_ref[:, ln] + apre_ref[:, ln])
        kk = k * kk_ref[:, ln]
        kk = kk / jnp.maximum(jnp.sqrt(head_sum(kk * kk)), 1e-12)
        k2 = k * (1.0 + (a - 1.0) * ka_ref[:, ln])
        bv = kk * a
        cl = lw
        d = 1
        while d < c:
            cl = cl + jnp.where(rowc >= d, pltpu.roll(cl, d, axis=0), 0.0)
            d *= 2
        prep.append((r, k2, v, kk, bv, lw, cl))

    items = [(p, ch) for p in range(npp) for ch in range(nch)]
    rt, at16, v16, lhs16, diag, a_ab, a_ak16 = {}, {}, {}, {}, {}, {}, {}
    for it in items:
        p, ch = it
        r, k2, v, kk, bv, lw, cl = prep[p]
        sl = slice(ch * c, (ch + 1) * c)
        clc = cl[sl]
        cle = clc[c - 1 : c, :]
        e_neg = jnp.exp(-clc)
        e_end = jnp.exp(cle - clc)
        rt_s = stack(r[sl] * jnp.exp(clc))
        at_s = stack(-kk[sl] * jnp.exp(clc - lw[sl]))
        gmat = _dot_nt(b16(jnp.concatenate([at_s, rt_s], axis=0)),
                       b16(jnp.concatenate([stack(bv[sl] * e_neg), stack(k2[sl] * e_neg)], axis=0)))
        a_ab[it] = gmat[:n2, :n2] * strict
        a_ak16[it] = b16(gmat[:n2, n2:] * strict)
        ends_t = jnp.concatenate([stack(bv[sl] * e_end).T, stack(k2[sl] * e_end).T], axis=1)
        lhs16[it] = b16(jnp.concatenate([gmat[n2:] * jnp.concatenate([incl, incl], axis=1), ends_t], axis=0))
        rt[it] = rt_s
        at16[it] = b16(at_s)
        v16[it] = b16(stack(v[sl]))
        diag[it] = eye * jnp.exp(cle)

    av16 = {it: b16(_dot(a_ak16[it], v16[it])) for it in items}
    tinv = {it: eye + a_ab[it] for it in items}
    pw = {it: _dot(b16(a_ab[it]), b16(a_ab[it])) for it in items}
    span = 2
    while span < c:
        span *= 2
        for it in items:
            if span < c:
                res = _dot(b16(pw[it]), b16(jnp.concatenate([tinv[it], pw[it]], axis=1)))
                tinv[it] = tinv[it] + res[:, :n2]
                pw[it] = res[:, n2:]
            else:
                tinv[it] = tinv[it] + _dot(b16(pw[it]), b16(tinv[it]))

    big = {}
    for it in items:
        wu16 = b16(_dot(b16(tinv[it]), jnp.concatenate([at16[it], av16[it]], axis=1)))
        rhs16 = jnp.concatenate([wu16, jnp.concatenate([b16(zeros_sq), v16[it]], axis=1)], axis=0)
        big[it] = _dot(lhs16[it], rhs16)

    st = {p: st_ref[p] for p in range(npp)}
    st_in = {}
    for ch in range(nch):
        for p in range(npp):
            it = (p, ch)
            st_in[it] = st[p]
            st[p] = _dot_bf16x3(diag[it] + big[it][n2:, :n2], st[p]) + big[it][n2:, n2:]
    for p in range(npp):
        st_ref[p] = st[p]

    for p in range(npp):
        ln = slice(p * n2, (p + 1) * n2)
        r, k2, v, kk, bv, lw, cl = prep[p]
        ys = []
        for ch in range(nch):
            it = (p, ch)
            y_s = _dot(b16(rt[it] + big[it][:n2, :n2]), b16(st_in[it])) + big[it][:n2, n2:]
            ys.append(y_s[:c] + y_s[c:])
        y = jnp.concatenate(ys, axis=0)
        mean = head_sum(y) * (1.0 / hd)
        yc = y - mean
        var = head_sum(yc * yc) * (1.0 / hd)
        yn = yc * lax.rsqrt(var + RWKV_LNX_EPS) * lg_ref[:, ln] + lb_ref[:, ln]
        bonus = head_sum(r * k2 * rk_ref[:, ln]) * v
        o_ref[:, ln] = ((yn + bonus) * g_ref[:, ln]).astype(o_ref.dtype)


def _rwkv_core(rkv, w_pre, a_pre, g, w0, a0, k_k, k_a, r_k, lnx_g, lnx_b, bsz, seqlen, tb=512, npp=2):
    _, n, d = rkv.shape
    tb = min(tb, seqlen)
    nt = seqlen // tb
    wl = npp * RWKV_PAIR
    rkv_spec = lambda j: pl.BlockSpec((1, tb, wl), lambda b, p, t: (j, b * nt + t, p))
    act_spec = pl.BlockSpec((tb, wl), lambda b, p, t: (b * nt + t, p))
    row_spec = pl.BlockSpec((1, wl), lambda b, p, t: (0, p))
    return pl.pallas_call(
        functools.partial(_rwkv_core_kernel, tb=tb, npp=npp),
        grid=(bsz, d // wl, nt),
        in_specs=[rkv_spec(0), rkv_spec(1), rkv_spec(2), act_spec, act_spec, act_spec] + [row_spec] * 7,
        out_specs=act_spec,
        out_shape=jax.ShapeDtypeStruct((n, d), BF16),
        scratch_shapes=[pltpu.VMEM((npp, RWKV_PAIR, RWKV_PAIR), F32)],
        compiler_params=_params(3),
        name="rwkv_core",
    )(rkv, rkv, rkv, w_pre, a_pre, g, w0, a0, k_k, k_a, r_k, lnx_g, lnx_b)


def _pool_kernel(x_ref, prev_ref, w_ref, sc_ref, g_ref, b_ref, o_ref, *, tm, seqlen, alpha):
    start = (pl.program_id(0) * tm) % seqlen
    x = x_ref[...]
    prev = jnp.where(start == 0, 0.0, prev_ref[...])
    xe = jnp.concatenate([prev, x], axis=0)
    pos = start + lax.broadcasted_iota(jnp.int32, (tm, 1), 0)
    gw = w_ref.shape[1]
    ys = []
    for gi, win in enumerate(POOL_WINDOWS):
        sl = slice(gi * gw, (gi + 1) * gw)
        s = xe[:, sl]
        d = 1
        while d < win:
            s = s + pltpu.roll(s, d, axis=0)
            d *= 2
        cnt = jnp.minimum(pos + 1, win).astype(F32)
        pooled = s[POOL_HALO:] / cnt - x[:, sl]
        ys.append(_dot(pooled.astype(BF16), w_ref[gi]))
    y = jnp.concatenate(ys, axis=1) * sc_ref[...]
    o_ref[...] = _layer_norm_rows(alpha * x + y, g_ref[...], b_ref[...])


def _pool_ln(x, w_pool, scale, g, b, seqlen, alpha, tm=512):
    n, d = x.shape
    tm = min(tm, seqlen)
    step = tm // POOL_HALO
    row_spec = pl.BlockSpec((1, d), lambda i: (0, 0))
    return pl.pallas_call(
        functools.partial(_pool_kernel, tm=tm, seqlen=seqlen, alpha=alpha),
        grid=(n // tm,),
        in_specs=[
            pl.BlockSpec((tm, d), lambda i: (i, 0)),
            pl.BlockSpec((POOL_HALO, d), lambda i: (jnp.maximum(i * step - 1, 0), 0)),
            pl.BlockSpec(w_pool.shape, lambda i: (0, 0, 0)),
            row_spec,
            row_spec,
            row_spec,
        ],
        out_specs=pl.BlockSpec((tm, d), lambda i: (i, 0)),
        out_shape=jax.ShapeDtypeStruct((n, d), F32),
        compiler_params=_params(1),
        name="pool_ln",
    )(x, x, w_pool, scale, g, b)


def kernel(x, ln_g, ln_b, mlp_w1, mlp_w2, rg_w_in, rg_conv_w, rg_conv_b, rg_gate_a_w, rg_gate_a_b, rg_gate_x_w, rg_gate_x_b, rg_lambda, rg_w_out, moba_w_qkv, moba_w_out, rwkv_mu, rwkv_w_rkv, rwkv_w0, rwkv_w_w1, rwkv_w_w2, rwkv_a0, rwkv_a_w1, rwkv_a_w2, rwkv_g_w1, rwkv_g_w2, rwkv_k_k, rwkv_k_a, rwkv_r_k, rwkv_lnx_g, rwkv_lnx_b, rwkv_w_out, pool_w, pool_scale):
    bsz, seqlen, d = x.shape
    depth = ln_g.shape[0]
    n_mixers = 4
    alpha = (2.0 * depth) ** 0.25
    bf = lambda w: w.astype(BF16)
    row = lambda p: p.reshape(1, -1)

    h = x.reshape(bsz * seqlen, d)
    counts = [0] * n_mixers
    for layer in range(depth):
        m = layer % n_mixers
        j = counts[m]
        counts[m] += 1
        g0, b0 = row(ln_g[layer, 0]), row(ln_b[layer, 0])
        if m == 0:
            proj = _matmul(h, bf(rg_w_in[j]), F32)
            mixed = _rglru(proj, rg_conv_w[j], row(rg_conv_b[j]), bf(rg_gate_a_w[j]), row(rg_gate_a_b[j]),
                           bf(rg_gate_x_w[j]), row(rg_gate_x_b[j]), row(rg_lambda[j]), bsz, seqlen)
            h = _matmul_ln(mixed, bf(rg_w_out[j]), h, g0, b0, alpha)
        elif m == 1:
            qkv = _qkv_rope(h, bf(moba_w_qkv[j]), seqlen)
            att = _moba_attention(qkv, bsz, seqlen)
            h = _matmul_ln(att, bf(moba_w_out[j]), h, g0, b0, alpha)
        elif m == 2:
            rkv = _rwkv_proj(h, rwkv_mu[j], bf(rwkv_w_rkv[j]), seqlen)
            w_pre, a_pre, gate = _rwkv_lora(h, rwkv_mu[j], bf(rwkv_w_w1[j]), bf(rwkv_w_w2[j]), bf(rwkv_a_w1[j]),
                                            bf(rwkv_a_w2[j]), bf(rwkv_g_w1[j]), bf(rwkv_g_w2[j]), seqlen)
            mixed = _rwkv_core(rkv, w_pre, a_pre, gate, row(rwkv_w0[j]), row(rwkv_a0[j]), row(rwkv_k_k[j]), row(rwkv_k_a[j]),
                               row(rwkv_r_k[j]), row(rwkv_lnx_g[j]), row(rwkv_lnx_b[j]), bsz, seqlen)
            h = _matmul_ln(mixed, bf(rwkv_w_out[j]), h, g0, b0, alpha)
        else:
            h = _pool_ln(h, bf(pool_w[j]), row(pool_scale[j]), g0, b0, seqlen, alpha)
        h = _mlp_ln(h, bf(mlp_w1[layer]), bf(mlp_w2[layer]), row(ln_g[layer, 1]), row(ln_b[layer, 1]), alpha)
    return h.reshape(bsz, seqlen, d)
```

```python
import functools

import jax
import jax.numpy as jnp
from jax import lax
from jax.experimental import pallas as pl
from jax.experimental.pallas import tpu as pltpu

F32 = jnp.float32
BF16 = jnp.bfloat16
HIGHEST = lax.Precision.HIGHEST

LN_EPS = 1e-5
NEG_INF = -1e30
VMEM_LIMIT_BYTES = 48 * 1024 * 1024
MLP_VMEM_LIMIT_BYTES = 58 * 1024 * 1024

LRU_C = 8.0
CONV_WIDTH = 4
ATTN_HEAD_DIM = 128
MOBA_BLOCK = 256
MOBA_TOPK = 3
ROPE_THETA = 500000.0
ROPE_DIM = ATTN_HEAD_DIM // 4
RWKV_HEAD_DIM = 64
RWKV_PAIR = 2 * RWKV_HEAD_DIM
RWKV_CHUNK = 64
RWKV_LNX_EPS = 64e-5
POOL_WINDOWS = (2, 4, 8, 16)
POOL_HALO = 16
LANES = 128
SUBLANES = 8


def _params(n_axes, vmem_limit_bytes=VMEM_LIMIT_BYTES):
    return pltpu.CompilerParams(dimension_semantics=("arbitrary",) * n_axes, vmem_limit_bytes=vmem_limit_bytes)


def _layer_norm_rows(z, g, b):
    mean = jnp.mean(z, axis=-1, keepdims=True)
    zc = z - mean
    var = jnp.mean(zc * zc, axis=-1, keepdims=True)
    return zc * lax.rsqrt(var + LN_EPS) * g + b


def _softplus(z):
    return jnp.maximum(z, 0.0) + jnp.log1p(jnp.exp(-jnp.abs(z)))


def _dot(a, b, precision=None):
    return jnp.dot(a, b, preferred_element_type=F32, precision=precision)


def _split_bf16(z):
    hi = z.astype(BF16)
    return hi, (z - hi.astype(F32)).astype(BF16)


def _dot_bf16x3(a, b):
    a_hi, a_lo = _split_bf16(a)
    b_hi, b_lo = _split_bf16(b)
    return _dot(jnp.concatenate([a_hi, a_lo], axis=1), jnp.concatenate([b_hi, b_hi], axis=0)) + _dot(a_hi, b_lo)


def _dot_nt(a, b, precision=None):
    return lax.dot_general(a, b, (((1,), (1,)), ((), ())), preferred_element_type=F32, precision=precision)


def _mm_kernel(a_ref, w_ref, o_ref):
    o_ref[...] = _dot(a_ref[...].astype(BF16), w_ref[...]).astype(o_ref.dtype)


def _matmul(a, w, out_dtype, tm=1024, tn=1024):
    n, k = a.shape
    nout = w.shape[1]
    tn = min(tn, nout)
    return pl.pallas_call(
        _mm_kernel,
        grid=(n // tm, nout // tn),
        in_specs=[pl.BlockSpec((tm, k), lambda i, j: (i, 0)), pl.BlockSpec((k, tn), lambda i, j: (0, j))],
        out_specs=pl.BlockSpec((tm, tn), lambda i, j: (i, j)),
        out_shape=jax.ShapeDtypeStruct((n, nout), out_dtype),
        compiler_params=_params(2),
        name="matmul",
    )(a, w)


def _mm_ln_kernel(a_ref, w_ref, x_ref, g_ref, b_ref, o_ref, acc_ref, *, alpha):
    @pl.when(pl.program_id(0) == 0)
    def _():
        acc_ref[...] = jnp.zeros_like(acc_ref)

    o_ref[...] = _layer_norm_rows(alpha * x_ref[...] + acc_ref[...], g_ref[...], b_ref[...])
    acc_ref[...] = _dot(a_ref[...].astype(BF16), w_ref[...])


def _matmul_ln(a, w, x, g, b, alpha, tm=512):
    n, k = a.shape
    d = w.shape[1]
    ni = n // tm
    cur = lambda s: jnp.minimum(s, ni - 1)
    lag = lambda s: jnp.maximum(s - 1, 0)
    return pl.pallas_call(
        functools.partial(_mm_ln_kernel, alpha=alpha),
        grid=(ni + 1,),
        in_specs=[
            pl.BlockSpec((tm, k), lambda s: (cur(s), 0)),
            pl.BlockSpec((k, d), lambda s: (0, 0)),
            pl.BlockSpec((tm, d), lambda s: (lag(s), 0)),
            pl.BlockSpec((1, d), lambda s: (0, 0)),
            pl.BlockSpec((1, d), lambda s: (0, 0)),
        ],
        out_specs=pl.BlockSpec((tm, d), lambda s: (lag(s), 0)),
        out_shape=jax.ShapeDtypeStruct((n, d), F32),
        scratch_shapes=[pltpu.VMEM((tm, d), F32)],
        compiler_params=_params(1),
        name="matmul_ln",
    )(a, w, x, g, b)


def _mlp_kernel(x_ref, w1_ref, w2_ref, g_ref, b_ref, o_ref, xb_ref, *, alpha):
    f = pl.program_id(1)

    @pl.when(f == 0)
    def _():
        xb_ref[...] = x_ref[...].astype(BF16)
        o_ref[...] = alpha * x_ref[...]

    h = jnp.maximum(_dot(xb_ref[...], w1_ref[...]), 0.0)
    o_ref[...] += _dot((h * h).astype(BF16), w2_ref[...])

    @pl.when(f == pl.num_programs(1) - 1)
    def _():
        o_ref[...] = _layer_norm_rows(o_ref[...], g_ref[...], b_ref[...])


def _mlp_ln(x, w1, w2, g, b, alpha, tm=512, tf=2048):
    n, d = x.shape
    dff = w1.shape[1]
    return pl.pallas_call(
        functools.partial(_mlp_kernel, alpha=alpha),
        grid=(n // tm, dff // tf),
        in_specs=[
            pl.BlockSpec((tm, d), lambda i, f: (i, 0)),
            pl.BlockSpec((d, tf), lambda i, f: (0, f)),
            pl.BlockSpec((tf, d), lambda i, f: (f, 0)),
            pl.BlockSpec((1, d), lambda i, f: (0, 0)),
            pl.BlockSpec((1, d), lambda i, f: (0, 0)),
        ],
        out_specs=pl.BlockSpec((tm, d), lambda i, f: (i, 0)),
        out_shape=jax.ShapeDtypeStruct((n, d), F32),
        scratch_shapes=[pltpu.VMEM((tm, d), BF16)],
        compiler_params=_params(2, MLP_VMEM_LIMIT_BYTES),
        name="mlp_ln",
    )(x, w1, w2, g, b)


def _rglru_kernel(gate_ref, u_ref, cw_ref, cb_ref, aw_ref, ab_ref, xw_ref, xb_ref, lam_ref, o_ref, hc_ref, ut_ref, *, ts):
    @pl.when(pl.program_id(2) == 0)
    def _():
        hc_ref[...] = jnp.zeros_like(hc_ref)
        ut_ref[...] = jnp.zeros_like(ut_ref)

    u = u_ref[...]
    ue = jnp.concatenate([ut_ref[...], u], axis=0)
    conv = cw_ref[CONV_WIDTH - 1 : CONV_WIDTH, :] * u + cb_ref[...]
    for d in range(1, CONV_WIDTH):
        conv = conv + cw_ref[CONV_WIDTH - 1 - d : CONV_WIDTH - d, :] * pltpu.roll(ue, d, axis=0)[SUBLANES:]
    ut_ref[...] = u[ts - SUBLANES :, :]

    cb16 = conv.astype(BF16)
    r = jax.nn.sigmoid(_dot(cb16, aw_ref[0]) + ab_ref[...])
    i = jax.nn.sigmoid(_dot(cb16, xw_ref[0]) + xb_ref[...])
    log_a = (-LRU_C) * r * _softplus(-lam_ref[...])
    a = jnp.exp(log_a)
    b = conv * i * jnp.sqrt(1.0 - a * a)

    w = a.shape[1]
    a = a.reshape(ts // SUBLANES, SUBLANES, w)
    b = b.reshape(ts // SUBLANES, SUBLANES, w)
    row = lax.broadcasted_iota(jnp.int32, a.shape, 1)
    d = 1
    while d < SUBLANES:
        keep = row >= d
        a_sh = jnp.where(keep, pltpu.roll(a, d, axis=1), 1.0)
        b_sh = jnp.where(keep, pltpu.roll(b, d, axis=1), 0.0)
        b = a * b_sh + b
        a = a * a_sh
        d *= 2
    h_prev = hc_ref[0:1, :]
    hs = []
    for g in range(ts // SUBLANES):
        hg = a[g] * h_prev + b[g]
        hs.append(hg)
        h_prev = hg[SUBLANES - 1 : SUBLANES, :]
    h = jnp.concatenate(hs, axis=0)
    hc_ref[...] = jnp.broadcast_to(h_prev, hc_ref.shape)
    o_ref[...] = (jax.nn.gelu(gate_ref[...]) * h).astype(o_ref.dtype)


def _rglru(proj, conv_w, conv_b, gate_a_w, gate_a_b, gate_x_w, gate_x_b, lam, bsz, seqlen, ts=1024):
    n, r2 = proj.shape
    r = r2 // 2
    nblk, w, _ = gate_a_w.shape
    ts = min(ts, seqlen)
    nt = seqlen // ts
    row_spec = pl.BlockSpec((1, w), lambda b, j, t: (0, j))
    gw_spec = pl.BlockSpec((1, w, w), lambda b, j, t: (j, 0, 0))
    return pl.pallas_call(
        functools.partial(_rglru_kernel, ts=ts),
        grid=(bsz, nblk, nt),
        in_specs=[
            pl.BlockSpec((ts, w), lambda b, j, t: (b * nt + t, j)),
            pl.BlockSpec((ts, w), lambda b, j, t: (b * nt + t, nblk + j)),
            pl.BlockSpec((CONV_WIDTH, w), lambda b, j, t: (0, j)),
            row_spec,
            gw_spec,
            row_spec,
            gw_spec,
            row_spec,
            row_spec,
        ],
        out_specs=pl.BlockSpec((ts, w), lambda b, j, t: (b * nt + t, j)),
        out_shape=jax.ShapeDtypeStruct((n, r), BF16),
        scratch_shapes=[pltpu.VMEM((SUBLANES, w), F32), pltpu.VMEM((SUBLANES, w), F32)],
        compiler_params=_params(3),
        name="rglru",
    )(proj, proj, conv_w, conv_b, gate_a_w, gate_a_b, gate_x_w, gate_x_b, lam)


def _qkv_kernel(x_ref, w_ref, c_ref, s1_ref, s2_ref, o_ref, acc_ref, xb_ref, *, tn, nj):
    @pl.when(pl.program_id(0) == 0)
    def _():
        acc_ref[...] = jnp.zeros_like(acc_ref)

    @pl.when(pl.program_id(0) % nj == 0)
    def _():
        xb_ref[...] = x_ref[...].astype(BF16)

    half = ROPE_DIM // 2
    rep = tn // ATTN_HEAD_DIM
    prev = acc_ref[...]
    c = jnp.concatenate([c_ref[0]] * rep, axis=1)
    s1 = jnp.concatenate([s1_ref[0]] * rep, axis=1)
    s2 = jnp.concatenate([s2_ref[0]] * rep, axis=1)
    roped = prev * c + pltpu.roll(prev, half, axis=1) * s1 + pltpu.roll(prev, tn - half, axis=1) * s2
    o_ref[...] = roped.astype(o_ref.dtype)
    acc_ref[...] = _dot(xb_ref[...], w_ref[...])


def _rope_tables(seqlen, scale):
    half = ROPE_DIM // 2
    pos = jnp.arange(seqlen, dtype=F32)
    inv_freq = ROPE_THETA ** (-jnp.arange(0, ROPE_DIM, 2, dtype=F32) / ROPE_DIM)
    ang = pos[:, None] * inv_freq[None, :]
    cos, sin = jnp.cos(ang), jnp.sin(ang)
    pad = ATTN_HEAD_DIM - ROPE_DIM
    c = jnp.concatenate([cos, cos, jnp.ones((seqlen, pad), F32)], axis=1)
    s1 = jnp.concatenate([jnp.zeros((seqlen, half), F32), sin, jnp.zeros((seqlen, pad), F32)], axis=1)
    s2 = jnp.concatenate([-sin, jnp.zeros((seqlen, half + pad), F32)], axis=1)
    ident = jnp.ones_like(c)
    zero = jnp.zeros_like(c)
    return (jnp.stack([c * scale, c, ident]), jnp.stack([s1 * scale, s1, zero]), jnp.stack([s2 * scale, s2, zero]))


def _qkv_rope(x, w_qkv, seqlen, tm=512, tn=512):
    n, d = x.shape
    nout = w_qkv.shape[1]
    tm = min(tm, seqlen)
    tn = min(tn, d)
    c, s1, s2 = _rope_tables(seqlen, ATTN_HEAD_DIM**-0.5)
    nt = seqlen // tm
    ni, nj = n // tm, nout // tn
    last = ni * nj - 1
    cur = lambda s: jnp.minimum(s, last)
    lag = lambda s: jnp.maximum(s - 1, 0)
    tab_spec = pl.BlockSpec((1, tm, ATTN_HEAD_DIM), lambda s: ((lag(s) % nj) * tn // d, (lag(s) // nj) % nt, 0))
    return pl.pallas_call(
        functools.partial(_qkv_kernel, tn=tn, nj=nj),
        grid=(ni * nj + 1,),
        in_specs=[
            pl.BlockSpec((tm, d), lambda s: (cur(s) // nj, 0)),
            pl.BlockSpec((d, tn), lambda s: (0, cur(s) % nj)),
            tab_spec,
            tab_spec,
            tab_spec,
        ],
        out_specs=pl.BlockSpec((tm, tn), lambda s: (lag(s) // nj, lag(s) % nj)),
        out_shape=jax.ShapeDtypeStruct((n, nout), BF16),
        scratch_shapes=[pltpu.VMEM((tm, tn), F32), pltpu.VMEM((tm, d), BF16)],
        compiler_params=_params(1),
        name="qkv_rope",
    )(x, w_qkv, c, s1, s2)


def _moba_kernel(q_ref, k_ref, v_ref, o_ref, kmean_ref, vt_ref, sel_ref, s_ref, *, nblk, nh):
    i = pl.program_id(2)
    blk, hd = MOBA_BLOCK, ATTN_HEAD_DIM

    heads = range(nh)
    lanes = [slice(h * hd, (h + 1) * hd) for h in heads]

    @pl.when(i == 0)
    def _():
        for h in heads:
            kf = k_ref[:, lanes[h]].astype(F32).reshape(nblk, blk, hd)
            kmean_ref[h] = jnp.sum(kf, axis=1) * (1.0 / blk)
            vt_ref[h] = v_ref[:, lanes[h]].astype(F32).T.astype(BF16)

    q = [q_ref[:, lanes[h]] for h in heads]

    def scores(jp, h):
        off = pl.multiple_of(jp * (2 * blk), 2 * blk)
        return _dot_nt(k_ref[pl.ds(off, 2 * blk), lanes[h]], q[h])

    for h in heads:
        s_ref[0, h] = scores(0, h)

    for h in heads:
        gate = _dot_nt(kmean_ref[h], q[h].astype(F32), precision=HIGHEST)
        blk_id = lax.broadcasted_iota(jnp.int32, gate.shape, 0)
        past = jnp.where(blk_id < i, 1.0, 0.0)
        rows = []
        for n in range(nblk):
            gn = gate[n : n + 1, :]
            tie = jnp.where(blk_id < n, 1.0, 0.0)
            beats = jnp.where(gate > gn, 1.0, jnp.where(gate == gn, tie, 0.0)) * past
            cnt = jnp.sum(beats, axis=0, keepdims=True)
            rows.append(jnp.where(cnt < MOBA_TOPK, 1.0, 0.0) * jnp.where(n < i, 1.0, 0.0))
        sel_ref[h] = jnp.concatenate(rows, axis=0)

    off_own = pl.multiple_of(i * blk, blk)
    key_pos = lax.broadcasted_iota(jnp.int32, (blk, blk), 0)
    qry_pos = lax.broadcasted_iota(jnp.int32, (blk, blk), 1)
    carry = []
    for h in heads:
        s = _dot_nt(k_ref[pl.ds(off_own, blk), lanes[h]], q[h])
        s = jnp.where(key_pos <= qry_pos, s, NEG_INF)
        m = jnp.max(s, axis=0, keepdims=True)
        p = jnp.exp(s - m)
        l = jnp.sum(p, axis=0, keepdims=True)
        acc = _dot(vt_ref[h, :, pl.ds(off_own, blk)], p.astype(BF16))
        carry.append((m, l, acc))

    def body(jp, carry):
        off = pl.multiple_of(jp * (2 * blk), 2 * blk)
        s_cur = [s_ref[jp % 2, h] for h in heads]
        jn = jnp.minimum(jp + 1, nblk // 2 - 1)
        for h in heads:
            s_ref[(jp + 1) % 2, h] = scores(jn, h)
        out = []
        for h in heads:
            m, l, acc = carry[h]
            s0 = jnp.where(sel_ref[h, pl.ds(2 * jp, 1), :] > 0.5, s_cur[h][:blk], NEG_INF)
            s1 = jnp.where(sel_ref[h, pl.ds(2 * jp + 1, 1), :] > 0.5, s_cur[h][blk:], NEG_INF)
            m_blk = jnp.maximum(jnp.max(s0, axis=0, keepdims=True), jnp.max(s1, axis=0, keepdims=True))
            m_new = jnp.maximum(m, m_blk)
            alpha = jnp.exp(m - m_new)
            p0 = jnp.exp(s0 - m_new)
            p1 = jnp.exp(s1 - m_new)
            l = alpha * l + jnp.sum(p0, axis=0, keepdims=True) + jnp.sum(p1, axis=0, keepdims=True)
            p = jnp.concatenate([p0.astype(BF16), p1.astype(BF16)], axis=0)
            acc = acc * alpha + _dot(vt_ref[h, :, pl.ds(off, 2 * blk)], p)
            out.append((m_new, l, acc))
        return tuple(out)

    carry = lax.fori_loop(0, (i + 1) // 2, body, tuple(carry))
    for h in heads:
        m, l, acc = carry[h]
        o_ref[:, lanes[h]] = (acc / l).T.astype(o_ref.dtype)


def _moba_attention(qkv, bsz, seqlen, nh=2):
    n, d3 = qkv.shape
    d = d3 // 3
    ngrp = d // (nh * ATTN_HEAD_DIM)
    nblk = seqlen // MOBA_BLOCK
    assert nblk % 2 == 0
    blk, hd = MOBA_BLOCK, ATTN_HEAD_DIM
    wl = nh * hd
    return pl.pallas_call(
        functools.partial(_moba_kernel, nblk=nblk, nh=nh),
        grid=(bsz, ngrp, nblk),
        in_specs=[
            pl.BlockSpec((blk, wl), lambda b, h, i: (b * nblk + i, h)),
            pl.BlockSpec((seqlen, wl), lambda b, h, i: (b, ngrp + h)),
            pl.BlockSpec((seqlen, wl), lambda b, h, i: (b, 2 * ngrp + h)),
        ],
        out_specs=pl.BlockSpec((blk, wl), lambda b, h, i: (b * nblk + i, h)),
        out_shape=jax.ShapeDtypeStruct((n, d), BF16),
        scratch_shapes=[pltpu.VMEM((nh, nblk, hd), F32), pltpu.VMEM((nh, hd, seqlen), BF16), pltpu.VMEM((nh, nblk, blk), F32),
                        pltpu.VMEM((2, nh, 2 * blk, blk), F32)],
        compiler_params=_params(3),
        name="moba_attention",
    )(qkv, qkv, qkv)


def _token_shift_delta(x, prev_ref, is_start):
    prev_row = jnp.where(is_start, 0.0, prev_ref[SUBLANES - 1 : SUBLANES, :])
    row = lax.broadcasted_iota(jnp.int32, x.shape, 0)
    return jnp.where(row == 0, prev_row, pltpu.roll(x, 1, axis=0)) - x


def _rwkv_proj_kernel(x_ref, prev_ref, mu_ref, w_ref, o_ref, *, tm, seqlen):
    x = x_ref[...]
    xx = _token_shift_delta(x, prev_ref, (pl.program_id(1) * tm) % seqlen == 0)
    o_ref[0] = _dot((x + xx * mu_ref[0]).astype(BF16), w_ref[0])


def _rwkv_proj(x, mu, w_rkv, seqlen, tm=512):
    n, d = x.shape
    tm = min(tm, seqlen)
    step = tm // SUBLANES
    return pl.pallas_call(
        functools.partial(_rwkv_proj_kernel, tm=tm, seqlen=seqlen),
        grid=(3, n // tm),
        in_specs=[
            pl.BlockSpec((tm, d), lambda g, i: (i, 0)),
            pl.BlockSpec((SUBLANES, d), lambda g, i: (jnp.maximum(i * step - 1, 0), 0)),
            pl.BlockSpec((1, 1, d), lambda g, i: (g, 0, 0)),
            pl.BlockSpec((1, d, d), lambda g, i: (g, 0, 0)),
        ],
        out_specs=pl.BlockSpec((1, tm, d), lambda g, i: (g, i, 0)),
        out_shape=jax.ShapeDtypeStruct((3, n, d), F32),
        compiler_params=_params(2),
        name="rwkv_proj",
    )(x, x, mu.reshape(mu.shape[0], 1, d), w_rkv)


def _rwkv_lora_kernel(x_ref, prev_ref, mu_ref, ww1_ref, ww2_ref, aw1_ref, aw2_ref, gw1_ref, gw2_ref, wo_ref, ao_ref, go_ref, *, tm, seqlen):
    x = x_ref[...]
    xx = _token_shift_delta(x, prev_ref, (pl.program_id(0) * tm) % seqlen == 0)
    xw = (x + xx * mu_ref[3:4, :]).astype(BF16)
    xa = (x + xx * mu_ref[4:5, :]).astype(BF16)
    xg = (x + xx * mu_ref[5:6, :]).astype(BF16)
    wo_ref[...] = _dot(jnp.tanh(_dot(xw, ww1_ref[...])).astype(BF16), ww2_ref[...])
    ao_ref[...] = _dot(_dot(xa, aw1_ref[...]).astype(BF16), aw2_ref[...])
    go_ref[...] = _dot(jax.nn.sigmoid(_dot(xg, gw1_ref[...])).astype(BF16), gw2_ref[...])


def _pad_lora(w1, w2):
    rank = w1.shape[1]
    pad = (-rank) % LANES
    return jnp.pad(w1, ((0, 0), (0, pad))), jnp.pad(w2, ((0, pad), (0, 0)))


def _rwkv_lora(x, mu, w_w1, w_w2, a_w1, a_w2, g_w1, g_w2, seqlen, tm=256):
    n, d = x.shape
    tm = min(tm, seqlen)
    step = tm // SUBLANES
    ws = [*_pad_lora(w_w1, w_w2), *_pad_lora(a_w1, a_w2), *_pad_lora(g_w1, g_w2)]
    full = lambda a: pl.BlockSpec(a.shape, lambda i: (0, 0))
    out_spec = pl.BlockSpec((tm, d), lambda i: (i, 0))
    return pl.pallas_call(
        functools.partial(_rwkv_lora_kernel, tm=tm, seqlen=seqlen),
        grid=(n // tm,),
        in_specs=[
            pl.BlockSpec((tm, d), lambda i: (i, 0)),
            pl.BlockSpec((SUBLANES, d), lambda i: (jnp.maximum(i * step - 1, 0), 0)),
            full(mu),
            *[full(w) for w in ws],
        ],
        out_specs=[out_spec, out_spec, out_spec],
        out_shape=[jax.ShapeDtypeStruct((n, d), F32)] * 3,
        compiler_params=_params(1),
        name="rwkv_lora",
    )(x, x, mu, *ws)


def _rwkv_core_kernel(r_ref, k_ref, v_ref, wpre_ref, apre_ref, g_ref, w0_ref, a0_ref, kk_ref, ka_ref, rk_ref, lg_ref, lb_ref,
                      o_ref, st_ref, *, tb, npp):
    c = RWKV_CHUNK
    hd = RWKV_HEAD_DIM

    @pl.when(pl.program_id(2) == 0)
    def _():
        st_ref[...] = jnp.zeros_like(st_ref)

    lane = lax.broadcasted_iota(jnp.int32, (1, RWKV_PAIR), 1)
    m0 = jnp.where(lane < hd, 1.0, 0.0)
    m1 = 1.0 - m0
    ri = lax.broadcasted_iota(jnp.int32, (RWKV_PAIR, RWKV_PAIR), 0)
    ci = lax.broadcasted_iota(jnp.int32, (RWKV_PAIR, RWKV_PAIR), 1)
    same = (ri < hd) == (ci < hd)
    ones_bd = jnp.where(same, 1.0, 0.0)
    strict = jnp.where(same & (ci < ri), 1.0, 0.0)
    incl = jnp.where(same & (ci <= ri), 1.0, 0.0)
    eye = jnp.where(ri == ci, 1.0, 0.0)

    def head_sum(z):
        hi, lo = _split_bf16(z)
        return _dot(jnp.concatenate([hi, lo], axis=1), ones2_b16)

    def stack(z):
        return jnp.concatenate([z * m0, z * m1], axis=0)

    def b16(z):
        return z.astype(BF16)

    ones2_b16 = b16(jnp.concatenate([ones_bd, ones_bd], axis=0))
    zeros_sq = jnp.zeros((RWKV_PAIR, RWKV_PAIR), F32)

    n2 = RWKV_PAIR
    nch = tb // c
    rowc = lax.broadcasted_iota(jnp.int32, (tb, n2), 0) % c

    prep = []
    for p in range(npp):
        ln = slice(p * n2, (p + 1) * n2)
        r = r_ref[0, :, ln]
        k = k_ref[0, :, ln]
        v = v_ref[0, :, ln]
        w_log = -_softplus(-(w0_ref[:, ln] + wpre_ref[:, ln])) - 0.5
        lw = -jnp.exp(w_log)
        a = jax.nn.sigmoid(a0_ref[:, ln] + apre_ref[:, ln])
        kk = k * kk_ref[:, ln]
        kk = kk / jnp.maximum(jnp.sqrt(head_sum(kk * kk)), 1e-12)
        k2 = k * (1.0 + (a - 1.0) * ka_ref[:, ln])
        bv = kk * a
        cl = lw
        d = 1
        while d < c:
            cl = cl + jnp.where(rowc >= d, pltpu.roll(cl, d, axis=0), 0.0)
            d *= 2
        prep.append((r, k2, v, kk, bv, lw, cl))

    items = [(p, ch) for p in range(npp) for ch in range(nch)]
    rt, at16, v16, lhs16, diag, a_ab, a_ak16 = {}, {}, {}, {}, {}, {}, {}
    for it in items:
        p, ch = it
        r, k2, v, kk, bv, lw, cl = prep[p]
        sl = slice(ch * c, (ch + 1) * c)
        clc = cl[sl]
        cle = clc[c - 1 : c, :]
        e_neg = jnp.exp(-clc)
        e_end = jnp.exp(cle - clc)
        rt_s = stack(r[sl] * jnp.exp(clc))
        at_s = stack(-kk[sl] * jnp.exp(clc - lw[sl]))
        gmat = _dot_nt(b16(jnp.concatenate([at_s, rt_s], axis=0)),
                       b16(jnp.concatenate([stack(bv[sl] * e_neg), stack(k2[sl] * e_neg)], axis=0)))
        a_ab[it] = gmat[:n2, :n2] * strict
        a_ak16[it] = b16(gmat[:n2, n2:] * strict)
        ends_t = jnp.concatenate([stack(bv[sl] * e_end).T, stack(k2[sl] * e_end).T], axis=1)
        lhs16[it] = b16(jnp.concatenate([gmat[n2:] * jnp.concatenate([incl, incl], axis=1), ends_t], axis=0))
        rt[it] = rt_s
        at16[it] = b16(at_s)
        v16[it] = b16(stack(v[sl]))
        diag[it] = eye * jnp.exp(cle)

    av16 = {it: b16(_dot(a_ak16[it], v16[it])) for it in items}
    tinv = {it: eye + a_ab[it] for it in items}
    pw = {it: _dot(b16(a_ab[it]), b16(a_ab[it])) for it in items}
    span = 2
    while span < c:
        span *= 2
        for it in items:
            if span < c:
                res = _dot(b16(pw[it]), b16(jnp.concatenate([tinv[it], pw[it]], axis=1)))
                tinv[it] = tinv[it] + res[:, :n2]
                pw[it] = res[:, n2:]
            else:
                tinv[it] = tinv[it] + _dot(b16(pw[it]), b16(tinv[it]))

    big = {}
    for it in items:
        wu16 = b16(_dot(b16(tinv[it]), jnp.concatenate([at16[it], av16[it]], axis=1)))
        rhs16 = jnp.concatenate([wu16, jnp.concatenate([b16(zeros_sq), v16[it]], axis=1)], axis=0)
        big[it] = _dot(lhs16[it], rhs16)

    st = {p: st_ref[p] for p in range(npp)}
    st_in = {}
    for ch in range(nch):
        for p in range(npp):
            it = (p, ch)
            st_in[it] = st[p]
            st[p] = _dot_bf16x3(diag[it] + big[it][n2:, :n2], st[p]) + big[it][n2:, n2:]
    for p in range(npp):
        st_ref[p] = st[p]

    for p in range(npp):
        ln = slice(p * n2, (p + 1) * n2)
        r, k2, v, kk, bv, lw, cl = prep[p]
        ys = []
        for ch in range(nch):
            it = (p, ch)
            y_s = _dot(b16(rt[it] + big[it][:n2, :n2]), b16(st_in[it])) + big[it][:n2, n2:]
            ys.append(y_s[:c] + y_s[c:])
        y = jnp.concatenate(ys, axis=0)
        mean = head_sum(y) * (1.0 / hd)
        yc = y - mean
        var = head_sum(yc * yc) * (1.0 / hd)
        yn = yc * lax.rsqrt(var + RWKV_LNX_EPS) * lg_ref[:, ln] + lb_ref[:, ln]
        bonus = head_sum(r * k2 * rk_ref[:, ln]) * v
        o_ref[:, ln] = ((yn + bonus) * g_ref[:, ln]).astype(o_ref.dtype)


def _rwkv_core(rkv, w_pre, a_pre, g, w0, a0, k_k, k_a, r_k, lnx_g, lnx_b, bsz, seqlen, tb=1024, npp=2):
    _, n, d = rkv.shape
    tb = min(tb, seqlen)
    nt = seqlen // tb
    wl = npp * RWKV_PAIR
    rkv_spec = lambda j: pl.BlockSpec((1, tb, wl), lambda b, p, t: (j, b * nt + t, p))
    act_spec = pl.BlockSpec((tb, wl), lambda b, p, t: (b * nt + t, p))
    row_spec = pl.BlockSpec((1, wl), lambda b, p, t: (0, p))
    return pl.pallas_call(
        functools.partial(_rwkv_core_kernel, tb=tb, npp=npp),
        grid=(bsz, d // wl, nt),
        in_specs=[rkv_spec(0), rkv_spec(1), rkv_spec(2), act_spec, act_spec, act_spec] + [row_spec] * 7,
        out_specs=act_spec,
        out_shape=jax.ShapeDtypeStruct((n, d), BF16),
        scratch_shapes=[pltpu.VMEM((npp, RWKV_PAIR, RWKV_PAIR), F32)],
        compiler_params=_params(3),
        name="rwkv_core",
    )(rkv, rkv, rkv, w_pre, a_pre, g, w0, a0, k_k, k_a, r_k, lnx_g, lnx_b)


def _pool_kernel(x_ref, prev_ref, w_ref, sc_ref, g_ref, b_ref, o_ref, *, tm, seqlen, alpha):
    start = (pl.program_id(0) * tm) % seqlen
    x = x_ref[...]
    prev = jnp.where(start == 0, 0.0, prev_ref[...])
    xe = jnp.concatenate([prev, x], axis=0)
    pos = start + lax.broadcasted_iota(jnp.int32, (tm, 1), 0)
    gw = w_ref.shape[1]
    ys = []
    for gi, win in enumerate(POOL_WINDOWS):
        sl = slice(gi * gw, (gi + 1) * gw)
        s = xe[:, sl]
        d = 1
        while d < win:
            s = s + pltpu.roll(s, d, axis=0)
            d *= 2
        cnt = jnp.minimum(pos + 1, win).astype(F32)
        pooled = s[POOL_HALO:] / cnt - x[:, sl]
        ys.append(_dot(pooled.astype(BF16), w_ref[gi]))
    y = jnp.concatenate(ys, axis=1) * sc_ref[...]
    o_ref[...] = _layer_norm_rows(alpha * x + y, g_ref[...], b_ref[...])


def _pool_ln(x, w_pool, scale, g, b, seqlen, alpha, tm=512):
    n, d = x.shape
    tm = min(tm, seqlen)
    step = tm // POOL_HALO
    row_spec = pl.BlockSpec((1, d), lambda i: (0, 0))
    return pl.pallas_call(
        functools.partial(_pool_kernel, tm=tm, seqlen=seqlen, alpha=alpha),
        grid=(n // tm,),
        in_specs=[
            pl.BlockSpec((tm, d), lambda i: (i, 0)),
            pl.BlockSpec((POOL_HALO, d), lambda i: (jnp.maximum(i * step - 1, 0), 0)),
            pl.BlockSpec(w_pool.shape, lambda i: (0, 0, 0)),
            row_spec,
            row_spec,
            row_spec,
        ],
        out_specs=pl.BlockSpec((tm, d), lambda i: (i, 0)),
        out_shape=jax.ShapeDtypeStruct((n, d), F32),
        compiler_params=_params(1),
        name="pool_ln",
    )(x, x, w_pool, scale, g, b)


def kernel(x, ln_g, ln_b, mlp_w1, mlp_w2, rg_w_in, rg_conv_w, rg_conv_b, rg_gate_a_w, rg_gate_a_b, rg_gate_x_w, rg_gate_x_b, rg_lambda, rg_w_out, moba_w_qkv, moba_w_out, rwkv_mu, rwkv_w_rkv, rwkv_w0, rwkv_w_w1, rwkv_w_w2, rwkv_a0, rwkv_a_w1, rwkv_a_w2, rwkv_g_w1, rwkv_g_w2, rwkv_k_k, rwkv_k_a, rwkv_r_k, rwkv_lnx_g, rwkv_lnx_b, rwkv_w_out, pool_w, pool_scale):
    bsz, seqlen, d = x.shape
    depth = ln_g.shape[0]
    n_mixers = 4
    alpha = (2.0 * depth) ** 0.25
    bf = lambda w: w.astype(BF16)
    row = lambda p: p.reshape(1, -1)

    h = x.reshape(bsz * seqlen, d)
    counts = [0] * n_mixers
    for layer in range(depth):
        m = layer % n_mixers
        j = counts[m]
        counts[m] += 1
        g0, b0 = row(ln_g[layer, 0]), row(ln_b[layer, 0])
        if m == 0:
            proj = _matmul(h, bf(rg_w_in[j]), F32)
            mixed = _rglru(proj, rg_conv_w[j], row(rg_conv_b[j]), bf(rg_gate_a_w[j]), row(rg_gate_a_b[j]),
                           bf(rg_gate_x_w[j]), row(rg_gate_x_b[j]), row(rg_lambda[j]), bsz, seqlen)
            h = _matmul_ln(mixed, bf(rg_w_out[j]), h, g0, b0, alpha)
        elif m == 1:
            qkv = _qkv_rope(h, bf(moba_w_qkv[j]), seqlen)
            att = _moba_attention(qkv, bsz, seqlen)
            h = _matmul_ln(att, bf(moba_w_out[j]), h, g0, b0, alpha)
        elif m == 2:
            rkv = _rwkv_proj(h, rwkv_mu[j], bf(rwkv_w_rkv[j]), seqlen)
            w_pre, a_pre, gate = _rwkv_lora(h, rwkv_mu[j], bf(rwkv_w_w1[j]), bf(rwkv_w_w2[j]), bf(rwkv_a_w1[j]),
                                            bf(rwkv_a_w2[j]), bf(rwkv_g_w1[j]), bf(rwkv_g_w2[j]), seqlen)
            mixed = _rwkv_core(rkv, w_pre, a_pre, gate, row(rwkv_w0[j]), row(rwkv_a0[j]), row(rwkv_k_k[j]), row(rwkv_k_a[j]),
                               row(rwkv_r_k[j]), row(rwkv_lnx_g[j]), row(rwkv_lnx_b[j]), bsz, seqlen)
            h = _matmul_ln(mixed, bf(rwkv_w_out[j]), h, g0, b0, alpha)
        else:
            h = _pool_ln(h, bf(pool_w[j]), row(pool_scale[j]), g0, b0, seqlen, alpha)
        h = _mlp_ln(h, bf(mlp_w1[layer]), bf(mlp_w2[layer]), row(ln_g[layer, 1]), row(ln_b[layer, 1]), alpha)
    return h.reshape(bsz, seqlen, d)
```

```python
import functools

import jax
import jax.numpy as jnp
from jax import lax
from jax.experimental import pallas as pl
from jax.experimental.pallas import tpu as pltpu

F32 = jnp.float32
BF16 = jnp.bfloat16
HIGHEST = lax.Precision.HIGHEST

LN_EPS = 1e-5
NEG_INF = -1e30
MOBA_TAKEN = -3e38
VMEM_LIMIT_BYTES = 48 * 1024 * 1024
MLP_VMEM_LIMIT_BYTES = 58 * 1024 * 1024

LRU_C = 8.0
CONV_WIDTH = 4
ATTN_HEAD_DIM = 128
MOBA_BLOCK = 256
MOBA_TOPK = 3
ROPE_THETA = 500000.0
ROPE_DIM = ATTN_HEAD_DIM // 4
RWKV_HEAD_DIM = 64
RWKV_PAIR = 2 * RWKV_HEAD_DIM
RWKV_CHUNK = 64
RWKV_LNX_EPS = 64e-5
POOL_WINDOWS = (2, 4, 8, 16)
POOL_HALO = 16
LANES = 128
SUBLANES = 8


def _params(n_axes, vmem_limit_bytes=VMEM_LIMIT_BYTES):
    return pltpu.CompilerParams(dimension_semantics=("arbitrary",) * n_axes, vmem_limit_bytes=vmem_limit_bytes)


def _layer_norm_rows(z, g, b):
    mean = jnp.mean(z, axis=-1, keepdims=True)
    zc = z - mean
    var = jnp.mean(zc * zc, axis=-1, keepdims=True)
    return zc * lax.rsqrt(var + LN_EPS) * g + b


def _softplus(z):
    return jnp.maximum(z, 0.0) + jnp.log1p(jnp.exp(-jnp.abs(z)))


def _dot(a, b, precision=None):
    return jnp.dot(a, b, preferred_element_type=F32, precision=precision)


def _split_bf16(z):
    hi = z.astype(BF16)
    return hi, (z - hi.astype(F32)).astype(BF16)


def _dot_bf16x3(a, b):
    a_hi, a_lo = _split_bf16(a)
    b_hi, b_lo = _split_bf16(b)
    return _dot(jnp.concatenate([a_hi, a_lo], axis=1), jnp.concatenate([b_hi, b_hi], axis=0)) + _dot(a_hi, b_lo)


def _dot_nt(a, b, precision=None):
    return lax.dot_general(a, b, (((1,), (1,)), ((), ())), preferred_element_type=F32, precision=precision)


def _mm_kernel(a_ref, w_ref, o_ref):
    o_ref[...] = _dot(a_ref[...].astype(BF16), w_ref[...]).astype(o_ref.dtype)


def _matmul(a, w, out_dtype, tm=1024, tn=1024):
    n, k = a.shape
    nout = w.shape[1]
    tn = min(tn, nout)
    return pl.pallas_call(
        _mm_kernel,
        grid=(n // tm, nout // tn),
        in_specs=[pl.BlockSpec((tm, k), lambda i, j: (i, 0)), pl.BlockSpec((k, tn), lambda i, j: (0, j))],
        out_specs=pl.BlockSpec((tm, tn), lambda i, j: (i, j)),
        out_shape=jax.ShapeDtypeStruct((n, nout), out_dtype),
        compiler_params=_params(2),
        name="matmul",
    )(a, w)


def _mm_ln_kernel(a_ref, w_ref, x_ref, g_ref, b_ref, o_ref, acc_ref, *, alpha):
    @pl.when(pl.program_id(0) == 0)
    def _():
        acc_ref[...] = jnp.zeros_like(acc_ref)

    o_ref[...] = _layer_norm_rows(alpha * x_ref[...] + acc_ref[...], g_ref[...], b_ref[...])
    acc_ref[...] = _dot(a_ref[...].astype(BF16), w_ref[...])


def _matmul_ln(a, w, x, g, b, alpha, tm=512):
    n, k = a.shape
    d = w.shape[1]
    ni = n // tm
    cur = lambda s: jnp.minimum(s, ni - 1)
    lag = lambda s: jnp.maximum(s - 1, 0)
    return pl.pallas_call(
        functools.partial(_mm_ln_kernel, alpha=alpha),
        grid=(ni + 1,),
        in_specs=[
            pl.BlockSpec((tm, k), lambda s: (cur(s), 0)),
            pl.BlockSpec((k, d), lambda s: (0, 0)),
            pl.BlockSpec((tm, d), lambda s: (lag(s), 0)),
            pl.BlockSpec((1, d), lambda s: (0, 0)),
            pl.BlockSpec((1, d), lambda s: (0, 0)),
        ],
        out_specs=pl.BlockSpec((tm, d), lambda s: (lag(s), 0)),
        out_shape=jax.ShapeDtypeStruct((n, d), F32),
        scratch_shapes=[pltpu.VMEM((tm, d), F32)],
        compiler_params=_params(1),
        name="matmul_ln",
    )(a, w, x, g, b)


def _mlp_kernel(x_ref, w1_ref, w2_ref, g_ref, b_ref, o_ref, xb_ref, *, alpha):
    f = pl.program_id(1)

    @pl.when(f == 0)
    def _():
        xb_ref[...] = x_ref[...].astype(BF16)
        o_ref[...] = alpha * x_ref[...]

    h = jnp.maximum(_dot(xb_ref[...], w1_ref[...]), 0.0)
    o_ref[...] += _dot((h * h).astype(BF16), w2_ref[...])

    @pl.when(f == pl.num_programs(1) - 1)
    def _():
        o_ref[...] = _layer_norm_rows(o_ref[...], g_ref[...], b_ref[...])


def _mlp_ln(x, w1, w2, g, b, alpha, tm=512, tf=2048):
    n, d = x.shape
    dff = w1.shape[1]
    return pl.pallas_call(
        functools.partial(_mlp_kernel, alpha=alpha),
        grid=(n // tm, dff // tf),
        in_specs=[
            pl.BlockSpec((tm, d), lambda i, f: (i, 0)),
            pl.BlockSpec((d, tf), lambda i, f: (0, f)),
            pl.BlockSpec((tf, d), lambda i, f: (f, 0)),
            pl.BlockSpec((1, d), lambda i, f: (0, 0)),
            pl.BlockSpec((1, d), lambda i, f: (0, 0)),
        ],
        out_specs=pl.BlockSpec((tm, d), lambda i, f: (i, 0)),
        out_shape=jax.ShapeDtypeStruct((n, d), F32),
        scratch_shapes=[pltpu.VMEM((tm, d), BF16)],
        compiler_params=_params(2, MLP_VMEM_LIMIT_BYTES),
        name="mlp_ln",
    )(x, w1, w2, g, b)


def _rglru_kernel(gate_ref, u_ref, cw_ref, cb_ref, aw_ref, ab_ref, xw_ref, xb_ref, lam_ref, o_ref, hc_ref, ut_ref, *, ts):
    @pl.when(pl.program_id(2) == 0)
    def _():
        hc_ref[...] = jnp.zeros_like(hc_ref)
        ut_ref[...] = jnp.zeros_like(ut_ref)

    u = u_ref[...]
    ue = jnp.concatenate([ut_ref[...], u], axis=0)
    conv = cw_ref[CONV_WIDTH - 1 : CONV_WIDTH, :] * u + cb_ref[...]
    for d in range(1, CONV_WIDTH):
        conv = conv + cw_ref[CONV_WIDTH - 1 - d : CONV_WIDTH - d, :] * pltpu.roll(ue, d, axis=0)[SUBLANES:]
    ut_ref[...] = u[ts - SUBLANES :, :]

    cb16 = conv.astype(BF16)
    r = jax.nn.sigmoid(_dot(cb16, aw_ref[0]) + ab_ref[...])
    i = jax.nn.sigmoid(_dot(cb16, xw_ref[0]) + xb_ref[...])
    log_a = (-LRU_C) * r * _softplus(-lam_ref[...])
    a = jnp.exp(log_a)
    b = conv * i * jnp.sqrt(1.0 - a * a)

    w = a.shape[1]
    a = a.reshape(ts // SUBLANES, SUBLANES, w)
    b = b.reshape(ts // SUBLANES, SUBLANES, w)
    row = lax.broadcasted_iota(jnp.int32, a.shape, 1)
    d = 1
    while d < SUBLANES:
        keep = row >= d
        a_sh = jnp.where(keep, pltpu.roll(a, d, axis=1), 1.0)
        b_sh = jnp.where(keep, pltpu.roll(b, d, axis=1), 0.0)
        b = a * b_sh + b
        a = a * a_sh
        d *= 2
    h_prev = hc_ref[0:1, :]
    hs = []
    for g in range(ts // SUBLANES):
        hg = a[g] * h_prev + b[g]
        hs.append(hg)
        h_prev = hg[SUBLANES - 1 : SUBLANES, :]
    h = jnp.concatenate(hs, axis=0)
    hc_ref[...] = jnp.broadcast_to(h_prev, hc_ref.shape)
    o_ref[...] = (jax.nn.gelu(gate_ref[...]) * h).astype(o_ref.dtype)


def _rglru(proj, conv_w, conv_b, gate_a_w, gate_a_b, gate_x_w, gate_x_b, lam, bsz, seqlen, ts=1024):
    n, r2 = proj.shape
    r = r2 // 2
    nblk, w, _ = gate_a_w.shape
    ts = min(ts, seqlen)
    nt = seqlen // ts
    row_spec = pl.BlockSpec((1, w), lambda b, j, t: (0, j))
    gw_spec = pl.BlockSpec((1, w, w), lambda b, j, t: (j, 0, 0))
    return pl.pallas_call(
        functools.partial(_rglru_kernel, ts=ts),
        grid=(bsz, nblk, nt),
        in_specs=[
            pl.BlockSpec((ts, w), lambda b, j, t: (b * nt + t, j)),
            pl.BlockSpec((ts, w), lambda b, j, t: (b * nt + t, nblk + j)),
            pl.BlockSpec((CONV_WIDTH, w), lambda b, j, t: (0, j)),
            row_spec,
            gw_spec,
            row_spec,
            gw_spec,
            row_spec,
            row_spec,
        ],
        out_specs=pl.BlockSpec((ts, w), lambda b, j, t: (b * nt + t, j)),
        out_shape=jax.ShapeDtypeStruct((n, r), BF16),
        scratch_shapes=[pltpu.VMEM((SUBLANES, w), F32), pltpu.VMEM((SUBLANES, w), F32)],
        compiler_params=_params(3),
        name="rglru",
    )(proj, proj, conv_w, conv_b, gate_a_w, gate_a_b, gate_x_w, gate_x_b, lam)


def _qkv_kernel(x_ref, w_ref, c_ref, s1_ref, s2_ref, o_ref, acc_ref, xb_ref, *, tn, nj):
    @pl.when(pl.program_id(0) == 0)
    def _():
        acc_ref[...] = jnp.zeros_like(acc_ref)

    @pl.when(pl.program_id(0) % nj == 0)
    def _():
        xb_ref[...] = x_ref[...].astype(BF16)

    half = ROPE_DIM // 2
    rep = tn // ATTN_HEAD_DIM
    prev = acc_ref[...]
    c = jnp.concatenate([c_ref[0]] * rep, axis=1)
    s1 = jnp.concatenate([s1_ref[0]] * rep, axis=1)
    s2 = jnp.concatenate([s2_ref[0]] * rep, axis=1)
    roped = prev * c + pltpu.roll(prev, half, axis=1) * s1 + pltpu.roll(prev, tn - half, axis=1) * s2
    o_ref[...] = roped.astype(o_ref.dtype)
    acc_ref[...] = _dot(xb_ref[...], w_ref[...])


def _rope_tables(seqlen, scale):
    half = ROPE_DIM // 2
    pos = jnp.arange(seqlen, dtype=F32)
    inv_freq = ROPE_THETA ** (-jnp.arange(0, ROPE_DIM, 2, dtype=F32) / ROPE_DIM)
    ang = pos[:, None] * inv_freq[None, :]
    cos, sin = jnp.cos(ang), jnp.sin(ang)
    pad = ATTN_HEAD_DIM - ROPE_DIM
    c = jnp.concatenate([cos, cos, jnp.ones((seqlen, pad), F32)], axis=1)
    s1 = jnp.concatenate([jnp.zeros((seqlen, half), F32), sin, jnp.zeros((seqlen, pad), F32)], axis=1)
    s2 = jnp.concatenate([-sin, jnp.zeros((seqlen, half + pad), F32)], axis=1)
    ident = jnp.ones_like(c)
    zero = jnp.zeros_like(c)
    return (jnp.stack([c * scale, c, ident]), jnp.stack([s1 * scale, s1, zero]), jnp.stack([s2 * scale, s2, zero]))


def _qkv_rope(x, w_qkv, seqlen, tm=512, tn=512):
    n, d = x.shape
    nout = w_qkv.shape[1]
    tm = min(tm, seqlen)
    tn = min(tn, d)
    c, s1, s2 = _rope_tables(seqlen, ATTN_HEAD_DIM**-0.5)
    nt = seqlen // tm
    ni, nj = n // tm, nout // tn
    last = ni * nj - 1
    cur = lambda s: jnp.minimum(s, last)
    lag = lambda s: jnp.maximum(s - 1, 0)
    tab_spec = pl.BlockSpec((1, tm, ATTN_HEAD_DIM), lambda s: ((lag(s) % nj) * tn // d, (lag(s) // nj) % nt, 0))
    return pl.pallas_call(
        functools.partial(_qkv_kernel, tn=tn, nj=nj),
        grid=(ni * nj + 1,),
        in_specs=[
            pl.BlockSpec((tm, d), lambda s: (cur(s) // nj, 0)),
            pl.BlockSpec((d, tn), lambda s: (0, cur(s) % nj)),
            tab_spec,
            tab_spec,
            tab_spec,
        ],
        out_specs=pl.BlockSpec((tm, tn), lambda s: (lag(s) // nj, lag(s) % nj)),
        out_shape=jax.ShapeDtypeStruct((n, nout), BF16),
        scratch_shapes=[pltpu.VMEM((tm, tn), F32), pltpu.VMEM((tm, d), BF16)],
        compiler_params=_params(1),
        name="qkv_rope",
    )(x, w_qkv, c, s1, s2)


def _moba_kernel(q_ref, k_ref, v_ref, o_ref, kmean_ref, vt_ref, sel_ref, s_ref, *, nblk, nh, nq):
    t = pl.program_id(2)
    blk, hd = MOBA_BLOCK, ATTN_HEAD_DIM

    heads = range(nh)
    tiles = range(nq)
    combos = [(a, h) for a in tiles for h in heads]
    lanes = [slice(h * hd, (h + 1) * hd) for h in heads]
    rows = [slice(a * blk, (a + 1) * blk) for a in tiles]
    own = [nq * t + a for a in tiles]

    @pl.when(t == 0)
    def _():
        for h in heads:
            kf = k_ref[:, lanes[h]].astype(F32).reshape(nblk, blk, hd)
            kmean_ref[h] = jnp.sum(kf, axis=1) * (1.0 / blk)
            vt_ref[h] = v_ref[:, lanes[h]].astype(F32).T.astype(BF16)

    q = {(a, h): q_ref[rows[a], lanes[h]] for a, h in combos}

    def scores(jp, a, h):
        off = pl.multiple_of(jp * (2 * blk), 2 * blk)
        return _dot_nt(k_ref[pl.ds(off, 2 * blk), lanes[h]], q[a, h])

    for a, h in combos:
        s_ref[a, 0, h] = scores(0, a, h)

    for a, h in combos:
        gate = _dot_nt(kmean_ref[h], q[a, h].astype(F32), precision=HIGHEST)
        blk_id = lax.broadcasted_iota(jnp.int32, gate.shape, 0).astype(F32)
        past = blk_id < own[a].astype(F32)
        g = jnp.where(past, gate, NEG_INF)
        sel = jnp.zeros_like(gate)
        for _ in range(MOBA_TOPK):
            top = jnp.max(g, axis=0, keepdims=True)
            first = jnp.min(jnp.where(g == top, blk_id, float(nblk)), axis=0, keepdims=True)
            hit = blk_id == first
            sel = jnp.where(hit, 1.0, sel)
            g = jnp.where(hit, MOBA_TAKEN, g)
        sel_ref[a, h] = jnp.where(past, sel, 0.0)

    key_pos = lax.broadcasted_iota(jnp.int32, (blk, blk), 0)
    qry_pos = lax.broadcasted_iota(jnp.int32, (blk, blk), 1)
    carry = {}
    for a, h in combos:
        off_own = pl.multiple_of(own[a] * blk, blk)
        s = _dot_nt(k_ref[pl.ds(off_own, blk), lanes[h]], q[a, h])
        s = jnp.where(key_pos <= qry_pos, s, NEG_INF)
        m = jnp.max(s, axis=0, keepdims=True)
        p = jnp.exp(s - m)
        l = jnp.sum(p, axis=0, keepdims=True)
        acc = _dot(vt_ref[h, :, pl.ds(off_own, blk)], p.astype(BF16))
        carry[a, h] = (m, l, acc)

    def make_body(a):
        def body(jp, carry):
            off = pl.multiple_of(jp * (2 * blk), 2 * blk)
            s_cur = [s_ref[a, jp % 2, h] for h in heads]
            jn = jnp.minimum(jp + 1, nblk // 2 - 1)
            for h in heads:
                s_ref[a, (jp + 1) % 2, h] = scores(jn, a, h)
            out = []
            for h in heads:
                m, l, acc = carry[h]
                s0 = jnp.where(sel_ref[a, h, pl.ds(2 * jp, 1), :] > 0.5, s_cur[h][:blk], NEG_INF)
                s1 = jnp.where(sel_ref[a, h, pl.ds(2 * jp + 1, 1), :] > 0.5, s_cur[h][blk:], NEG_INF)
                m_blk = jnp.maximum(jnp.max(s0, axis=0, keepdims=True), jnp.max(s1, axis=0, keepdims=True))
                m_new = jnp.maximum(m, m_blk)
                alpha = jnp.exp(m - m_new)
                p0 = jnp.exp(s0 - m_new)
                p1 = jnp.exp(s1 - m_new)
                l = alpha * l + jnp.sum(p0, axis=0, keepdims=True) + jnp.sum(p1, axis=0, keepdims=True)
                p = jnp.concatenate([p0.astype(BF16), p1.astype(BF16)], axis=0)
                acc = acc * alpha + _dot(vt_ref[h, :, pl.ds(off, 2 * blk)], p)
                out.append((m_new, l, acc))
            return tuple(out)

        return body

    done = {}
    for a in tiles:
        res = lax.fori_loop(0, (own[a] + 1) // 2, make_body(a), tuple(carry[a, h] for h in heads))
        for h in heads:
            done[a, h] = res[h]
    for a, h in combos:
        m, l, acc = done[a, h]
        o_ref[rows[a], lanes[h]] = (acc / l).T.astype(o_ref.dtype)


def _moba_attention(qkv, bsz, seqlen, nh=2, nq=2):
    n, d3 = qkv.shape
    d = d3 // 3
    ngrp = d // (nh * ATTN_HEAD_DIM)
    nblk = seqlen // MOBA_BLOCK
    assert nblk % 2 == 0 and nblk % nq == 0
    blk, hd = MOBA_BLOCK, ATTN_HEAD_DIM
    wl = nh * hd
    nt = nblk // nq
    return pl.pallas_call(
        functools.partial(_moba_kernel, nblk=nblk, nh=nh, nq=nq),
        grid=(bsz, ngrp, nt),
        in_specs=[
            pl.BlockSpec((nq * blk, wl), lambda b, h, t: (b * nt + t, h)),
            pl.BlockSpec((seqlen, wl), lambda b, h, t: (b, ngrp + h)),
            pl.BlockSpec((seqlen, wl), lambda b, h, t: (b, 2 * ngrp + h)),
        ],
        out_specs=pl.BlockSpec((nq * blk, wl), lambda b, h, t: (b * nt + t, h)),
        out_shape=jax.ShapeDtypeStruct((n, d), BF16),
        scratch_shapes=[pltpu.VMEM((nh, nblk, hd), F32), pltpu.VMEM((nh, hd, seqlen), BF16),
                        pltpu.VMEM((nq, nh, nblk, blk), F32), pltpu.VMEM((nq, 2, nh, 2 * blk, blk), F32)],
        compiler_params=_params(3),
        name="moba_attention",
    )(qkv, qkv, qkv)


def _token_shift_delta(x, prev_ref, is_start):
    prev_row = jnp.where(is_start, 0.0, prev_ref[SUBLANES - 1 : SUBLANES, :])
    row = lax.broadcasted_iota(jnp.int32, x.shape, 0)
    return jnp.where(row == 0, prev_row, pltpu.roll(x, 1, axis=0)) - x


def _rwkv_proj_kernel(x_ref, prev_ref, mu_ref, w_ref, o_ref, *, tm, seqlen):
    x = x_ref[...]
    xx = _token_shift_delta(x, prev_ref, (pl.program_id(1) * tm) % seqlen == 0)
    o_ref[0] = _dot((x + xx * mu_ref[0]).astype(BF16), w_ref[0])


def _rwkv_proj(x, mu, w_rkv, seqlen, tm=512):
    n, d = x.shape
    tm = min(tm, seqlen)
    step = tm // SUBLANES
    return pl.pallas_call(
        functools.partial(_rwkv_proj_kernel, tm=tm, seqlen=seqlen),
        grid=(3, n // tm),
        in_specs=[
            pl.BlockSpec((tm, d), lambda g, i: (i, 0)),
            pl.BlockSpec((SUBLANES, d), lambda g, i: (jnp.maximum(i * step - 1, 0), 0)),
            pl.BlockSpec((1, 1, d), lambda g, i: (g, 0, 0)),
            pl.BlockSpec((1, d, d), lambda g, i: (g, 0, 0)),
        ],
        out_specs=pl.BlockSpec((1, tm, d), lambda g, i: (g, i, 0)),
        out_shape=jax.ShapeDtypeStruct((3, n, d), F32),
        compiler_params=_params(2),
        name="rwkv_proj",
    )(x, x, mu.reshape(mu.shape[0], 1, d), w_rkv)


def _rwkv_lora_kernel(x_ref, prev_ref, mu_ref, ww1_ref, ww2_ref, aw1_ref, aw2_ref, gw1_ref, gw2_ref, wo_ref, ao_ref, go_ref, *, tm, seqlen):
    x = x_ref[...]
    xx = _token_shift_delta(x, prev_ref, (pl.program_id(0) * tm) % seqlen == 0)
    xw = (x + xx * mu_ref[3:4, :]).astype(BF16)
    xa = (x + xx * mu_ref[4:5, :]).astype(BF16)
    xg = (x + xx * mu_ref[5:6, :]).astype(BF16)
    wo_ref[...] = _dot(jnp.tanh(_dot(xw, ww1_ref[...])).astype(BF16), ww2_ref[...])
    ao_ref[...] = _dot(_dot(xa, aw1_ref[...]).astype(BF16), aw2_ref[...])
    go_ref[...] = _dot(jax.nn.sigmoid(_dot(xg, gw1_ref[...])).astype(BF16), gw2_ref[...])


def _pad_lora(w1, w2):
    rank = w1.shape[1]
    pad = (-rank) % LANES
    return jnp.pad(w1, ((0, 0), (0, pad))), jnp.pad(w2, ((0, pad), (0, 0)))


def _rwkv_lora(x, mu, w_w1, w_w2, a_w1, a_w2, g_w1, g_w2, seqlen, tm=256):
    n, d = x.shape
    tm = min(tm, seqlen)
    step = tm // SUBLANES
    ws = [*_pad_lora(w_w1, w_w2), *_pad_lora(a_w1, a_w2), *_pad_lora(g_w1, g_w2)]
    full = lambda a: pl.BlockSpec(a.shape, lambda i: (0, 0))
    out_spec = pl.BlockSpec((tm, d), lambda i: (i, 0))
    return pl.pallas_call(
        functools.partial(_rwkv_lora_kernel, tm=tm, seqlen=seqlen),
        grid=(n // tm,),
        in_specs=[
            pl.BlockSpec((tm, d), lambda i: (i, 0)),
            pl.BlockSpec((SUBLANES, d), lambda i: (jnp.maximum(i * step - 1, 0), 0)),
            full(mu),
            *[full(w) for w in ws],
        ],
        out_specs=[out_spec, out_spec, out_spec],
        out_shape=[jax.ShapeDtypeStruct((n, d), F32)] * 3,
        compiler_params=_params(1),
        name="rwkv_lora",
    )(x, x, mu, *ws)


def _rwkv_core_kernel(r_ref, k_ref, v_ref, wpre_ref, apre_ref, g_ref, w0_ref, a0_ref, kk_ref, ka_ref, rk_ref, lg_ref, lb_ref,
                      o_ref, st_ref, *, tb, npp):
    c = RWKV_CHUNK
    hd = RWKV_HEAD_DIM

    @pl.when(pl.program_id(2) == 0)
    def _():
        st_ref[...] = jnp.zeros_like(st_ref)

    lane = lax.broadcasted_iota(jnp.int32, (1, RWKV_PAIR), 1)
    m0 = jnp.where(lane < hd, 1.0, 0.0)
    m1 = 1.0 - m0
    ri = lax.broadcasted_iota(jnp.int32, (RWKV_PAIR, RWKV_PAIR), 0)
    ci = lax.broadcasted_iota(jnp.int32, (RWKV_PAIR, RWKV_PAIR), 1)
    same = (ri < hd) == (ci < hd)
    ones_bd = jnp.where(same, 1.0, 0.0)
    strict = jnp.where(same & (ci < ri), 1.0, 0.0)
    incl = jnp.where(same & (ci <= ri), 1.0, 0.0)
    eye = jnp.where(ri == ci, 1.0, 0.0)

    def head_sum(z):
        hi, lo = _split_bf16(z)
        return _dot(jnp.concatenate([hi, lo], axis=1), ones2_b16)

    def stack(z):
        return jnp.concatenate([z * m0, z * m1], axis=0)

    def b16(z):
        return z.astype(BF16)

    ones2_b16 = b16(jnp.concatenate([ones_bd, ones_bd], axis=0))
    zeros_sq = jnp.zeros((RWKV_PAIR, RWKV_PAIR), F32)

    n2 = RWKV_PAIR
    nch = tb // c
    rowc = lax.broadcasted_iota(jnp.int32, (tb, n2), 0) % c

    prep = []
    for p in range(npp):
        ln = slice(p * n2, (p + 1) * n2)
        r = r_ref[0, :, ln]
        k = k_ref[0, :, ln]
        v = v_ref[0, :, ln]
        z = w0_ref[:, ln] + wpre_ref[:, ln]
        w_log = jnp.minimum(z, 0.0) - jnp.log(1.0 + jnp.exp(-jnp.abs(z))) - 0.5
        lw = -jnp.exp(w_log)
        a = jax.nn.sigmoid(a0_ref[:, ln] + apre_ref[:, ln])
        kk = k * kk_ref[:, ln]
        kk = kk / jnp.maximum(jnp.sqrt(head_sum(kk * kk)), 1e-12)
        k2 = k * (1.0 + (a - 1.0) * ka_ref[:, ln])
        bv = kk * a
        cl = lw
        d = 1
        while d < c:
            cl = cl + jnp.where(rowc >= d, pltpu.roll(cl, d, axis=0), 0.0)
            d *= 2
        prep.append((r, k2, v, kk, bv, lw, cl))

    items = [(p, ch) for p in range(npp) for ch in range(nch)]
    rt, at16, v16, lhs16, diag, a_ab, a_ak16 = {}, {}, {}, {}, {}, {}, {}
    for it in items:
        p, ch = it
        r, k2, v, kk, bv, lw, cl = prep[p]
        sl = slice(ch * c, (ch + 1) * c)
        clc = cl[sl]
        cle = clc[c - 1 : c, :]
        e_neg = jnp.exp(-clc)
        e_end = jnp.exp(cle - clc)
        rt_s = stack(r[sl] * jnp.exp(clc))
        at_s = stack(-kk[sl] * jnp.exp(clc - lw[sl]))
        gmat = _dot_nt(b16(jnp.concatenate([at_s, rt_s], axis=0)),
                       b16(jnp.concatenate([stack(bv[sl] * e_neg), stack(k2[sl] * e_neg)], axis=0)))
        a_ab[it] = gmat[:n2, :n2] * strict
        a_ak16[it] = b16(gmat[:n2, n2:] * strict)
        ends_t = jnp.concatenate([stack(bv[sl] * e_end).T, stack(k2[sl] * e_end).T], axis=1)
        lhs16[it] = b16(jnp.concatenate([gmat[n2:] * jnp.concatenate([incl, incl], axis=1), ends_t], axis=0))
        rt[it] = rt_s
        at16[it] = b16(at_s)
        v16[it] = b16(stack(v[sl]))
        diag[it] = eye * jnp.exp(cle)

    av16 = {it: b16(_dot(a_ak16[it], v16[it])) for it in items}
    tinv = {it: eye + a_ab[it] for it in items}
    l16 = {it: b16(a_ab[it]) for it in items}
    pw = {it: _dot(l16[it], l16[it]) for it in items}
    span = 2
    while span < c:
        span *= 2
        for it in items:
            p16 = b16(pw[it])
            if span < c:
                res = _dot(p16, jnp.concatenate([b16(tinv[it]), p16], axis=1))
                tinv[it] = tinv[it] + res[:, :n2]
                pw[it] = res[:, n2:]
            else:
                tinv[it] = tinv[it] + _dot(p16, b16(tinv[it]))

    big = {}
    for it in items:
        wu16 = b16(_dot(b16(tinv[it]), jnp.concatenate([at16[it], av16[it]], axis=1)))
        rhs16 = jnp.concatenate([wu16, jnp.concatenate([b16(zeros_sq), v16[it]], axis=1)], axis=0)
        big[it] = _dot(lhs16[it], rhs16)

    st = {p: st_ref[p] for p in range(npp)}
    st_in = {}
    for ch in range(nch):
        for p in range(npp):
            it = (p, ch)
            st_in[it] = st[p]
            st[p] = _dot_bf16x3(diag[it] + big[it][n2:, :n2], st[p]) + big[it][n2:, n2:]
    for p in range(npp):
        st_ref[p] = st[p]

    for p in range(npp):
        ln = slice(p * n2, (p + 1) * n2)
        r, k2, v, kk, bv, lw, cl = prep[p]
        ys = []
        for ch in range(nch):
            it = (p, ch)
            y_s = _dot(b16(rt[it] + big[it][:n2, :n2]), b16(st_in[it])) + big[it][:n2, n2:]
            ys.append(y_s[:c] + y_s[c:])
        y = jnp.concatenate(ys, axis=0)
        mean = head_sum(y) * (1.0 / hd)
        yc = y - mean
        var = head_sum(yc * yc) * (1.0 / hd)
        yn = yc * lax.rsqrt(var + RWKV_LNX_EPS) * lg_ref[:, ln] + lb_ref[:, ln]
        bonus = head_sum(r * k2 * rk_ref[:, ln]) * v
        o_ref[:, ln] = ((yn + bonus) * g_ref[:, ln]).astype(o_ref.dtype)


def _rwkv_core(rkv, w_pre, a_pre, g, w0, a0, k_k, k_a, r_k, lnx_g, lnx_b, bsz, seqlen, tb=1024, npp=2):
    _, n, d = rkv.shape
    tb = min(tb, seqlen)
    nt = seqlen // tb
    wl = npp * RWKV_PAIR
    rkv_spec = lambda j: pl.BlockSpec((1, tb, wl), lambda b, p, t: (j, b * nt + t, p))
    act_spec = pl.BlockSpec((tb, wl), lambda b, p, t: (b * nt + t, p))
    row_spec = pl.BlockSpec((1, wl), lambda b, p, t: (0, p))
    return pl.pallas_call(
        functools.partial(_rwkv_core_kernel, tb=tb, npp=npp),
        grid=(bsz, d // wl, nt),
        in_specs=[rkv_spec(0), rkv_spec(1), rkv_spec(2), act_spec, act_spec, act_spec] + [row_spec] * 7,
        out_specs=act_spec,
        out_shape=jax.ShapeDtypeStruct((n, d), BF16),
        scratch_shapes=[pltpu.VMEM((npp, RWKV_PAIR, RWKV_PAIR), F32)],
        compiler_params=_params(3),
        name="rwkv_core",
    )(rkv, rkv, rkv, w_pre, a_pre, g, w0, a0, k_k, k_a, r_k, lnx_g, lnx_b)


def _pool_kernel(x_ref, prev_ref, w_ref, sc_ref, g_ref, b_ref, o_ref, *, tm, seqlen, alpha):
    start = (pl.program_id(0) * tm) % seqlen
    x = x_ref[...]
    prev = jnp.where(start == 0, 0.0, prev_ref[...])
    xe = jnp.concatenate([prev, x], axis=0)
    pos = start + lax.broadcasted_iota(jnp.int32, (tm, 1), 0)
    gw = w_ref.shape[1]
    ys = []
    for gi, win in enumerate(POOL_WINDOWS):
        sl = slice(gi * gw, (gi + 1) * gw)
        s = xe[:, sl]
        d = 1
        while d < win:
            s = s + pltpu.roll(s, d, axis=0)
            d *= 2
        cnt = jnp.minimum(pos + 1, win).astype(F32)
        pooled = s[POOL_HALO:] / cnt - x[:, sl]
        ys.append(_dot(pooled.astype(BF16), w_ref[gi]))
    y = jnp.concatenate(ys, axis=1) * sc_ref[...]
    o_ref[...] = _layer_norm_rows(alpha * x + y, g_ref[...], b_ref[...])


def _pool_ln(x, w_pool, scale, g, b, seqlen, alpha, tm=512):
    n, d = x.shape
    tm = min(tm, seqlen)
    step = tm // POOL_HALO
    row_spec = pl.BlockSpec((1, d), lambda i: (0, 0))
    return pl.pallas_call(
        functools.partial(_pool_kernel, tm=tm, seqlen=seqlen, alpha=alpha),
        grid=(n // tm,),
        in_specs=[
            pl.BlockSpec((tm, d), lambda i: (i, 0)),
            pl.BlockSpec((POOL_HALO, d), lambda i: (jnp.maximum(i * step - 1, 0), 0)),
            pl.BlockSpec(w_pool.shape, lambda i: (0, 0, 0)),
            row_spec,
            row_spec,
            row_spec,
        ],
        out_specs=pl.BlockSpec((tm, d), lambda i: (i, 0)),
        out_shape=jax.ShapeDtypeStruct((n, d), F32),
        compiler_params=_params(1),
        name="pool_ln",
    )(x, x, w_pool, scale, g, b)


def kernel(x, ln_g, ln_b, mlp_w1, mlp_w2, rg_w_in, rg_conv_w, rg_conv_b, rg_gate_a_w, rg_gate_a_b, rg_gate_x_w, rg_gate_x_b, rg_lambda, rg_w_out, moba_w_qkv, moba_w_out, rwkv_mu, rwkv_w_rkv, rwkv_w0, rwkv_w_w1, rwkv_w_w2, rwkv_a0, rwkv_a_w1, rwkv_a_w2, rwkv_g_w1, rwkv_g_w2, rwkv_k_k, rwkv_k_a, rwkv_r_k, rwkv_lnx_g, rwkv_lnx_b, rwkv_w_out, pool_w, pool_scale):
    bsz, seqlen, d = x.shape
    depth = ln_g.shape[0]
    n_mixers = 4
    alpha = (2.0 * depth) ** 0.25
    bf = lambda w: w.astype(BF16)
    row = lambda p: p.reshape(1, -1)

    h = x.reshape(bsz * seqlen, d)
    counts = [0] * n_mixers
    for layer in range(depth):
        m = layer % n_mixers
        j = counts[m]
        counts[m] += 1
        g0, b0 = row(ln_g[layer, 0]), row(ln_b[layer, 0])
        if m == 0:
            proj = _matmul(h, bf(rg_w_in[j]), F32)
            mixed = _rglru(proj, rg_conv_w[j], row(rg_conv_b[j]), bf(rg_gate_a_w[j]), row(rg_gate_a_b[j]),
                           bf(rg_gate_x_w[j]), row(rg_gate_x_b[j]), row(rg_lambda[j]), bsz, seqlen)
            h = _matmul_ln(mixed, bf(rg_w_out[j]), h, g0, b0, alpha)
        elif m == 1:
            qkv = _qkv_rope(h, bf(moba_w_qkv[j]), seqlen)
            att = _moba_attention(qkv, bsz, seqlen)
            h = _matmul_ln(att, bf(moba_w_out[j]), h, g0, b0, alpha)
        elif m == 2:
            rkv = _rwkv_proj(h, rwkv_mu[j], bf(rwkv_w_rkv[j]), seqlen)
            w_pre, a_pre, gate = _rwkv_lora(h, rwkv_mu[j], bf(rwkv_w_w1[j]), bf(rwkv_w_w2[j]), bf(rwkv_a_w1[j]),
                                            bf(rwkv_a_w2[j]), bf(rwkv_g_w1[j]), bf(rwkv_g_w2[j]), seqlen)
            mixed = _rwkv_core(rkv, w_pre, a_pre, gate, row(rwkv_w0[j]), row(rwkv_a0[j]), row(rwkv_k_k[j]), row(rwkv_k_a[j]),
                               row(rwkv_r_k[j]), row(rwkv_lnx_g[j]), row(rwkv_lnx_b[j]), bsz, seqlen)
            h = _matmul_ln(mixed, bf(rwkv_w_out[j]), h, g0, b0, alpha)
        else:
            h = _pool_ln(h, bf(pool_w[j]), row(pool_scale[j]), g0, b0, seqlen, alpha)
        h = _mlp_ln(h, bf(mlp_w1[layer]), bf(mlp_w2[layer]), row(ln_g[layer, 1]), row(ln_b[layer, 1]), alpha)
    return h.reshape(bsz, seqlen, d)
```

```python
import functools

import jax
import jax.numpy as jnp
from jax import lax
from jax.experimental import pallas as pl
from jax.experimental.pallas import tpu as pltpu

F32 = jnp.float32
BF16 = jnp.bfloat16
HIGHEST = lax.Precision.HIGHEST

LN_EPS = 1e-5
NEG_INF = -1e30
MOBA_TAKEN = -3e38
VMEM_LIMIT_BYTES = 48 * 1024 * 1024
MLP_VMEM_LIMIT_BYTES = 58 * 1024 * 1024

LRU_C = 8.0
CONV_WIDTH = 4
ATTN_HEAD_DIM = 128
MOBA_BLOCK = 256
MOBA_TOPK = 3
ROPE_THETA = 500000.0
ROPE_DIM = ATTN_HEAD_DIM // 4
RWKV_HEAD_DIM = 64
RWKV_PAIR = 2 * RWKV_HEAD_DIM
RWKV_CHUNK = 64
RWKV_LNX_EPS = 64e-5
POOL_WINDOWS = (2, 4, 8, 16)
POOL_HALO = 16
LANES = 128
SUBLANES = 8


def _params(n_axes, vmem_limit_bytes=VMEM_LIMIT_BYTES):
    return pltpu.CompilerParams(dimension_semantics=("arbitrary",) * n_axes, vmem_limit_bytes=vmem_limit_bytes)


def _layer_norm_rows(z, g, b):
    mean = jnp.mean(z, axis=-1, keepdims=True)
    zc = z - mean
    var = jnp.mean(zc * zc, axis=-1, keepdims=True)
    return zc * lax.rsqrt(var + LN_EPS) * g + b


def _softplus(z):
    return jnp.maximum(z, 0.0) + jnp.log1p(jnp.exp(-jnp.abs(z)))


def _dot(a, b, precision=None):
    return jnp.dot(a, b, preferred_element_type=F32, precision=precision)


def _split_bf16(z):
    hi = z.astype(BF16)
    return hi, (z - hi.astype(F32)).astype(BF16)


def _dot_bf16x3(a, b):
    a_hi, a_lo = _split_bf16(a)
    b_hi, b_lo = _split_bf16(b)
    return _dot(jnp.concatenate([a_hi, a_lo], axis=1), jnp.concatenate([b_hi, b_hi], axis=0)) + _dot(a_hi, b_lo)


def _dot_nt(a, b, precision=None):
    return lax.dot_general(a, b, (((1,), (1,)), ((), ())), preferred_element_type=F32, precision=precision)


def _mm_kernel(a_ref, w_ref, o_ref):
    o_ref[...] = _dot(a_ref[...].astype(BF16), w_ref[...]).astype(o_ref.dtype)


def _matmul(a, w, out_dtype, tm=1024, tn=1024):
    n, k = a.shape
    nout = w.shape[1]
    tn = min(tn, nout)
    return pl.pallas_call(
        _mm_kernel,
        grid=(n // tm, nout // tn),
        in_specs=[pl.BlockSpec((tm, k), lambda i, j: (i, 0)), pl.BlockSpec((k, tn), lambda i, j: (0, j))],
        out_specs=pl.BlockSpec((tm, tn), lambda i, j: (i, j)),
        out_shape=jax.ShapeDtypeStruct((n, nout), out_dtype),
        compiler_params=_params(2),
        name="matmul",
    )(a, w)


def _mm_ln_kernel(a_ref, w_ref, x_ref, g_ref, b_ref, o_ref, acc_ref, *, alpha):
    @pl.when(pl.program_id(0) == 0)
    def _():
        acc_ref[...] = jnp.zeros_like(acc_ref)

    o_ref[...] = _layer_norm_rows(alpha * x_ref[...] + acc_ref[...], g_ref[...], b_ref[...])
    acc_ref[...] = _dot(a_ref[...].astype(BF16), w_ref[...])


def _matmul_ln(a, w, x, g, b, alpha, tm=512):
    n, k = a.shape
    d = w.shape[1]
    ni = n // tm
    cur = lambda s: jnp.minimum(s, ni - 1)
    lag = lambda s: jnp.maximum(s - 1, 0)
    return pl.pallas_call(
        functools.partial(_mm_ln_kernel, alpha=alpha),
        grid=(ni + 1,),
        in_specs=[
            pl.BlockSpec((tm, k), lambda s: (cur(s), 0)),
            pl.BlockSpec((k, d), lambda s: (0, 0)),
            pl.BlockSpec((tm, d), lambda s: (lag(s), 0)),
            pl.BlockSpec((1, d), lambda s: (0, 0)),
            pl.BlockSpec((1, d), lambda s: (0, 0)),
        ],
        out_specs=pl.BlockSpec((tm, d), lambda s: (lag(s), 0)),
        out_shape=jax.ShapeDtypeStruct((n, d), F32),
        scratch_shapes=[pltpu.VMEM((tm, d), F32)],
        compiler_params=_params(1),
        name="matmul_ln",
    )(a, w, x, g, b)


def _mlp_kernel(x_ref, w1_ref, w2_ref, g_ref, b_ref, o_ref, xb_ref, *, alpha, nf):
    f = pl.program_id(1)

    def partial_out(xb):
        h = jnp.maximum(_dot(xb, w1_ref[...]), 0.0)
        return _dot((h * h).astype(BF16), w2_ref[...])

    def first():
        xb = x_ref[...].astype(BF16)
        xb_ref[...] = xb
        return alpha * x_ref[...] + partial_out(xb)

    if nf == 1:
        o_ref[...] = _layer_norm_rows(first(), g_ref[...], b_ref[...])
        return

    @pl.when(f == 0)
    def _():
        o_ref[...] = first()

    @pl.when(jnp.logical_and(f > 0, f < nf - 1))
    def _():
        o_ref[...] += partial_out(xb_ref[...])

    @pl.when(f == nf - 1)
    def _():
        o_ref[...] = _layer_norm_rows(o_ref[...] + partial_out(xb_ref[...]), g_ref[...], b_ref[...])


def _mlp_ln(x, w1, w2, g, b, alpha, tm=512, tf=2048):
    n, d = x.shape
    dff = w1.shape[1]
    return pl.pallas_call(
        functools.partial(_mlp_kernel, alpha=alpha, nf=dff // tf),
        grid=(n // tm, dff // tf),
        in_specs=[
            pl.BlockSpec((tm, d), lambda i, f: (i, 0)),
            pl.BlockSpec((d, tf), lambda i, f: (0, f)),
            pl.BlockSpec((tf, d), lambda i, f: (f, 0)),
            pl.BlockSpec((1, d), lambda i, f: (0, 0)),
            pl.BlockSpec((1, d), lambda i, f: (0, 0)),
        ],
        out_specs=pl.BlockSpec((tm, d), lambda i, f: (i, 0)),
        out_shape=jax.ShapeDtypeStruct((n, d), F32),
        scratch_shapes=[pltpu.VMEM((tm, d), BF16)],
        compiler_params=_params(2, MLP_VMEM_LIMIT_BYTES),
        name="mlp_ln",
    )(x, w1, w2, g, b)


def _rglru_kernel(gate_ref, u_ref, cw_ref, cb_ref, aw_ref, ab_ref, xw_ref, xb_ref, lam_ref, o_ref, hc_ref, ut_ref, *, ts):
    @pl.when(pl.program_id(2) == 0)
    def _():
        hc_ref[...] = jnp.zeros_like(hc_ref)
        ut_ref[...] = jnp.zeros_like(ut_ref)

    u = u_ref[...]
    ue = jnp.concatenate([ut_ref[...], u], axis=0)
    conv = cw_ref[CONV_WIDTH - 1 : CONV_WIDTH, :] * u + cb_ref[...]
    for d in range(1, CONV_WIDTH):
        conv = conv + cw_ref[CONV_WIDTH - 1 - d : CONV_WIDTH - d, :] * pltpu.roll(ue, d, axis=0)[SUBLANES:]
    ut_ref[...] = u[ts - SUBLANES :, :]

    cb16 = conv.astype(BF16)
    r = jax.nn.sigmoid(_dot(cb16, aw_ref[0]) + ab_ref[...])
    i = jax.nn.sigmoid(_dot(cb16, xw_ref[0]) + xb_ref[...])
    log_a = (-LRU_C) * r * _softplus(-lam_ref[...])
    a = jnp.exp(log_a)
    b = conv * i * jnp.sqrt(1.0 - a * a)

    w = a.shape[1]
    a = a.reshape(ts // SUBLANES, SUBLANES, w)
    b = b.reshape(ts // SUBLANES, SUBLANES, w)
    row = lax.broadcasted_iota(jnp.int32, a.shape, 1)
    d = 1
    while d < SUBLANES:
        keep = row >= d
        a_sh = jnp.where(keep, pltpu.roll(a, d, axis=1), 1.0)
        b_sh = jnp.where(keep, pltpu.roll(b, d, axis=1), 0.0)
        b = a * b_sh + b
        a = a * a_sh
        d *= 2
    h_prev = hc_ref[0:1, :]
    hs = []
    for g in range(ts // SUBLANES):
        hg = a[g] * h_prev + b[g]
        hs.append(hg)
        h_prev = hg[SUBLANES - 1 : SUBLANES, :]
    h = jnp.concatenate(hs, axis=0)
    hc_ref[...] = jnp.broadcast_to(h_prev, hc_ref.shape)
    o_ref[...] = (jax.nn.gelu(gate_ref[...]) * h).astype(o_ref.dtype)


def _rglru(proj, conv_w, conv_b, gate_a_w, gate_a_b, gate_x_w, gate_x_b, lam, bsz, seqlen, ts=1024):
    n, r2 = proj.shape
    r = r2 // 2
    nblk, w, _ = gate_a_w.shape
    ts = min(ts, seqlen)
    nt = seqlen // ts
    row_spec = pl.BlockSpec((1, w), lambda b, j, t: (0, j))
    gw_spec = pl.BlockSpec((1, w, w), lambda b, j, t: (j, 0, 0))
    return pl.pallas_call(
        functools.partial(_rglru_kernel, ts=ts),
        grid=(bsz, nblk, nt),
        in_specs=[
            pl.BlockSpec((ts, w), lambda b, j, t: (b * nt + t, j)),
            pl.BlockSpec((ts, w), lambda b, j, t: (b * nt + t, nblk + j)),
            pl.BlockSpec((CONV_WIDTH, w), lambda b, j, t: (0, j)),
            row_spec,
            gw_spec,
            row_spec,
            gw_spec,
            row_spec,
            row_spec,
        ],
        out_specs=pl.BlockSpec((ts, w), lambda b, j, t: (b * nt + t, j)),
        out_shape=jax.ShapeDtypeStruct((n, r), BF16),
        scratch_shapes=[pltpu.VMEM((SUBLANES, w), F32), pltpu.VMEM((SUBLANES, w), F32)],
        compiler_params=_params(3),
        name="rglru",
    )(proj, proj, conv_w, conv_b, gate_a_w, gate_a_b, gate_x_w, gate_x_b, lam)


def _qkv_kernel(x_ref, w_ref, c_ref, s1_ref, s2_ref, o_ref, acc_ref, xb_ref, *, tn, nj):
    @pl.when(pl.program_id(0) == 0)
    def _():
        acc_ref[...] = jnp.zeros_like(acc_ref)

    @pl.when(pl.program_id(0) % nj == 0)
    def _():
        xb_ref[...] = x_ref[...].astype(BF16)

    half = ROPE_DIM // 2
    rep = tn // ATTN_HEAD_DIM
    prev = acc_ref[...]
    c = jnp.concatenate([c_ref[0]] * rep, axis=1)
    s1 = jnp.concatenate([s1_ref[0]] * rep, axis=1)
    s2 = jnp.concatenate([s2_ref[0]] * rep, axis=1)
    roped = prev * c + pltpu.roll(prev, half, axis=1) * s1 + pltpu.roll(prev, tn - half, axis=1) * s2
    o_ref[...] = roped.astype(o_ref.dtype)
    acc_ref[...] = _dot(xb_ref[...], w_ref[...])


def _rope_tables(seqlen, scale):
    half = ROPE_DIM // 2
    pos = jnp.arange(seqlen, dtype=F32)
    inv_freq = ROPE_THETA ** (-jnp.arange(0, ROPE_DIM, 2, dtype=F32) / ROPE_DIM)
    ang = pos[:, None] * inv_freq[None, :]
    cos, sin = jnp.cos(ang), jnp.sin(ang)
    pad = ATTN_HEAD_DIM - ROPE_DIM
    c = jnp.concatenate([cos, cos, jnp.ones((seqlen, pad), F32)], axis=1)
    s1 = jnp.concatenate([jnp.zeros((seqlen, half), F32), sin, jnp.zeros((seqlen, pad), F32)], axis=1)
    s2 = jnp.concatenate([-sin, jnp.zeros((seqlen, half + pad), F32)], axis=1)
    ident = jnp.ones_like(c)
    zero = jnp.zeros_like(c)
    return (jnp.stack([c * scale, c, ident]), jnp.stack([s1 * scale, s1, zero]), jnp.stack([s2 * scale, s2, zero]))


def _qkv_rope(x, w_qkv, seqlen, tm=512, tn=1024):
    n, d = x.shape
    nout = w_qkv.shape[1]
    tm = min(tm, seqlen)
    tn = min(tn, d)
    c, s1, s2 = _rope_tables(seqlen, ATTN_HEAD_DIM**-0.5)
    nt = seqlen // tm
    ni, nj = n // tm, nout // tn
    last = ni * nj - 1
    cur = lambda s: jnp.minimum(s, last)
    lag = lambda s: jnp.maximum(s - 1, 0)
    tab_spec = pl.BlockSpec((1, tm, ATTN_HEAD_DIM), lambda s: ((lag(s) % nj) * tn // d, (lag(s) // nj) % nt, 0))
    return pl.pallas_call(
        functools.partial(_qkv_kernel, tn=tn, nj=nj),
        grid=(ni * nj + 1,),
        in_specs=[
            pl.BlockSpec((tm, d), lambda s: (cur(s) // nj, 0)),
            pl.BlockSpec((d, tn), lambda s: (0, cur(s) % nj)),
            tab_spec,
            tab_spec,
            tab_spec,
        ],
        out_specs=pl.BlockSpec((tm, tn), lambda s: (lag(s) // nj, lag(s) % nj)),
        out_shape=jax.ShapeDtypeStruct((n, nout), BF16),
        scratch_shapes=[pltpu.VMEM((tm, tn), F32), pltpu.VMEM((tm, d), BF16)],
        compiler_params=_params(1),
        name="qkv_rope",
    )(x, w_qkv, c, s1, s2)


def _moba_kernel(q_ref, k_ref, v_ref, o_ref, kmean_ref, vt_ref, sel_ref, s_ref, *, nblk, nh, nq):
    t = pl.program_id(2)
    blk, hd = MOBA_BLOCK, ATTN_HEAD_DIM

    heads = range(nh)
    tiles = range(nq)
    combos = [(a, h) for a in tiles for h in heads]
    lanes = [slice(h * hd, (h + 1) * hd) for h in heads]
    rows = [slice(a * blk, (a + 1) * blk) for a in tiles]
    own = [nq * t + a for a in tiles]

    @pl.when(t == 0)
    def _():
        for h in heads:
            kf = k_ref[:, lanes[h]].astype(F32).reshape(nblk, blk, hd)
            kmean_ref[h] = jnp.sum(kf, axis=1) * (1.0 / blk)
            vt_ref[h] = v_ref[:, lanes[h]].astype(F32).T.astype(BF16)

    q = {(a, h): q_ref[rows[a], lanes[h]] for a, h in combos}

    def scores(jp, a, h):
        off = pl.multiple_of(jp * (2 * blk), 2 * blk)
        return _dot_nt(k_ref[pl.ds(off, 2 * blk), lanes[h]], q[a, h])

    for a, h in combos:
        s_ref[a, 0, h] = scores(0, a, h)

    for a, h in combos:
        gate = _dot_nt(kmean_ref[h], q[a, h].astype(F32), precision=HIGHEST)
        blk_id = lax.broadcasted_iota(jnp.int32, gate.shape, 0).astype(F32)
        past = blk_id < own[a].astype(F32)
        g = jnp.where(past, gate, NEG_INF)
        sel = jnp.zeros_like(gate)
        for _ in range(MOBA_TOPK):
            top = jnp.max(g, axis=0, keepdims=True)
            first = jnp.min(jnp.where(g == top, blk_id, float(nblk)), axis=0, keepdims=True)
            hit = blk_id == first
            sel = jnp.where(hit, 1.0, sel)
            g = jnp.where(hit, MOBA_TAKEN, g)
        sel_ref[a, h] = jnp.where(past, sel, 0.0)

    key_pos = lax.broadcasted_iota(jnp.int32, (blk, blk), 0)
    qry_pos = lax.broadcasted_iota(jnp.int32, (blk, blk), 1)
    carry = {}
    for a, h in combos:
        off_own = pl.multiple_of(own[a] * blk, blk)
        s = _dot_nt(k_ref[pl.ds(off_own, blk), lanes[h]], q[a, h])
        s = jnp.where(key_pos <= qry_pos, s, NEG_INF)
        m = jnp.max(s, axis=0, keepdims=True)
        p = jnp.exp(s - m)
        l = jnp.sum(p, axis=0, keepdims=True)
        acc = _dot(vt_ref[h, :, pl.ds(off_own, blk)], p.astype(BF16))
        carry[a, h] = (m, l, acc)

    def make_body(a):
        def body(jp, carry):
            off = pl.multiple_of(jp * (2 * blk), 2 * blk)
            s_cur = [s_ref[a, jp % 2, h] for h in heads]
            jn = jnp.minimum(jp + 1, nblk // 2 - 1)
            for h in heads:
                s_ref[a, (jp + 1) % 2, h] = scores(jn, a, h)
            out = []
            for h in heads:
                m, l, acc = carry[h]
                s0 = jnp.where(sel_ref[a, h, pl.ds(2 * jp, 1), :] > 0.5, s_cur[h][:blk], NEG_INF)
                s1 = jnp.where(sel_ref[a, h, pl.ds(2 * jp + 1, 1), :] > 0.5, s_cur[h][blk:], NEG_INF)
                m_blk = jnp.maximum(jnp.max(s0, axis=0, keepdims=True), jnp.max(s1, axis=0, keepdims=True))
                m_new = jnp.maximum(m, m_blk)
                alpha = jnp.exp(m - m_new)
                p0 = jnp.exp(s0 - m_new)
                p1 = jnp.exp(s1 - m_new)
                l = alpha * l + jnp.sum(p0, axis=0, keepdims=True) + jnp.sum(p1, axis=0, keepdims=True)
                p = jnp.concatenate([p0.astype(BF16), p1.astype(BF16)], axis=0)
                acc = acc * alpha + _dot(vt_ref[h, :, pl.ds(off, 2 * blk)], p)
                out.append((m_new, l, acc))
            return tuple(out)

        return body

    done = {}
    for a in tiles:
        res = lax.fori_loop(0, (own[a] + 1) // 2, make_body(a), tuple(carry[a, h] for h in heads))
        for h in heads:
            done[a, h] = res[h]
    for a, h in combos:
        m, l, acc = done[a, h]
        o_ref[rows[a], lanes[h]] = (acc / l).T.astype(o_ref.dtype)


def _moba_attention(qkv, bsz, seqlen, nh=2, nq=2):
    n, d3 = qkv.shape
    d = d3 // 3
    ngrp = d // (nh * ATTN_HEAD_DIM)
    nblk = seqlen // MOBA_BLOCK
    assert nblk % 2 == 0 and nblk % nq == 0
    blk, hd = MOBA_BLOCK, ATTN_HEAD_DIM
    wl = nh * hd
    nt = nblk // nq
    return pl.pallas_call(
        functools.partial(_moba_kernel, nblk=nblk, nh=nh, nq=nq),
        grid=(bsz, ngrp, nt),
        in_specs=[
            pl.BlockSpec((nq * blk, wl), lambda b, h, t: (b * nt + t, h)),
            pl.BlockSpec((seqlen, wl), lambda b, h, t: (b, ngrp + h)),
            pl.BlockSpec((seqlen, wl), lambda b, h, t: (b, 2 * ngrp + h)),
        ],
        out_specs=pl.BlockSpec((nq * blk, wl), lambda b, h, t: (b * nt + t, h)),
        out_shape=jax.ShapeDtypeStruct((n, d), BF16),
        scratch_shapes=[pltpu.VMEM((nh, nblk, hd), F32), pltpu.VMEM((nh, hd, seqlen), BF16),
                        pltpu.VMEM((nq, nh, nblk, blk), F32), pltpu.VMEM((nq, 2, nh, 2 * blk, blk), F32)],
        compiler_params=_params(3),
        name="moba_attention",
    )(qkv, qkv, qkv)


def _token_shift_delta(x, prev_ref, is_start):
    prev_row = jnp.where(is_start, 0.0, prev_ref[SUBLANES - 1 : SUBLANES, :])
    row = lax.broadcasted_iota(jnp.int32, x.shape, 0)
    return jnp.where(row == 0, prev_row, pltpu.roll(x, 1, axis=0)) - x


def _rwkv_proj_kernel(x_ref, prev_ref, mu_ref, w_ref, o_ref, *, tm, seqlen):
    x = x_ref[...]
    xx = _token_shift_delta(x, prev_ref, (pl.program_id(1) * tm) % seqlen == 0)
    o_ref[0] = _dot((x + xx * mu_ref[0]).astype(BF16), w_ref[0])


def _rwkv_proj(x, mu, w_rkv, seqlen, tm=512):
    n, d = x.shape
    tm = min(tm, seqlen)
    step = tm // SUBLANES
    return pl.pallas_call(
        functools.partial(_rwkv_proj_kernel, tm=tm, seqlen=seqlen),
        grid=(3, n // tm),
        in_specs=[
            pl.BlockSpec((tm, d), lambda g, i: (i, 0)),
            pl.BlockSpec((SUBLANES, d), lambda g, i: (jnp.maximum(i * step - 1, 0), 0)),
            pl.BlockSpec((1, 1, d), lambda g, i: (g, 0, 0)),
            pl.BlockSpec((1, d, d), lambda g, i: (g, 0, 0)),
        ],
        out_specs=pl.BlockSpec((1, tm, d), lambda g, i: (g, i, 0)),
        out_shape=jax.ShapeDtypeStruct((3, n, d), F32),
        compiler_params=_params(2),
        name="rwkv_proj",
    )(x, x, mu.reshape(mu.shape[0], 1, d), w_rkv)


def _rwkv_lora_kernel(x_ref, prev_ref, mu_ref, ww1_ref, ww2_ref, aw1_ref, aw2_ref, gw1_ref, gw2_ref, wo_ref, ao_ref, go_ref, *, tm, seqlen):
    x = x_ref[...]
    xx = _token_shift_delta(x, prev_ref, (pl.program_id(0) * tm) % seqlen == 0)
    xw = (x + xx * mu_ref[3:4, :]).astype(BF16)
    xa = (x + xx * mu_ref[4:5, :]).astype(BF16)
    xg = (x + xx * mu_ref[5:6, :]).astype(BF16)
    wo_ref[...] = _dot(jnp.tanh(_dot(xw, ww1_ref[...])).astype(BF16), ww2_ref[...])
    ao_ref[...] = _dot(_dot(xa, aw1_ref[...]).astype(BF16), aw2_ref[...])
    go_ref[...] = _dot(jax.nn.sigmoid(_dot(xg, gw1_ref[...])).astype(BF16), gw2_ref[...])


def _pad_lora(w1, w2):
    rank = w1.shape[1]
    pad = (-rank) % LANES
    return jnp.pad(w1, ((0, 0), (0, pad))), jnp.pad(w2, ((0, pad), (0, 0)))


def _rwkv_lora(x, mu, w_w1, w_w2, a_w1, a_w2, g_w1, g_w2, seqlen, tm=512):
    n, d = x.shape
    tm = min(tm, seqlen)
    step = tm // SUBLANES
    ws = [*_pad_lora(w_w1, w_w2), *_pad_lora(a_w1, a_w2), *_pad_lora(g_w1, g_w2)]
    full = lambda a: pl.BlockSpec(a.shape, lambda i: (0, 0))
    out_spec = pl.BlockSpec((tm, d), lambda i: (i, 0))
    return pl.pallas_call(
        functools.partial(_rwkv_lora_kernel, tm=tm, seqlen=seqlen),
        grid=(n // tm,),
        in_specs=[
            pl.BlockSpec((tm, d), lambda i: (i, 0)),
            pl.BlockSpec((SUBLANES, d), lambda i: (jnp.maximum(i * step - 1, 0), 0)),
            full(mu),
            *[full(w) for w in ws],
        ],
        out_specs=[out_spec, out_spec, out_spec],
        out_shape=[jax.ShapeDtypeStruct((n, d), F32)] * 3,
        compiler_params=_params(1),
        name="rwkv_lora",
    )(x, x, mu, *ws)


def _rwkv_core_kernel(r_ref, k_ref, v_ref, wpre_ref, apre_ref, g_ref, w0_ref, a0_ref, kk_ref, ka_ref, rk_ref, lg_ref, lb_ref,
                      o_ref, st_ref, *, tb, npp):
    c = RWKV_CHUNK
    hd = RWKV_HEAD_DIM

    @pl.when(pl.program_id(2) == 0)
    def _():
        st_ref[...] = jnp.zeros_like(st_ref)

    lane = lax.broadcasted_iota(jnp.int32, (1, RWKV_PAIR), 1)
    m0 = jnp.where(lane < hd, 1.0, 0.0)
    m1 = 1.0 - m0
    ri = lax.broadcasted_iota(jnp.int32, (RWKV_PAIR, RWKV_PAIR), 0)
    ci = lax.broadcasted_iota(jnp.int32, (RWKV_PAIR, RWKV_PAIR), 1)
    same = (ri < hd) == (ci < hd)
    ones_bd = jnp.where(same, 1.0, 0.0)
    strict = jnp.where(same & (ci < ri), 1.0, 0.0)
    incl = jnp.where(same & (ci <= ri), 1.0, 0.0)
    eye = jnp.where(ri == ci, 1.0, 0.0)

    def head_sum(z):
        hi, lo = _split_bf16(z)
        return _dot(jnp.concatenate([hi, lo], axis=1), ones2_b16)

    def stack(z):
        return jnp.concatenate([z * m0, z * m1], axis=0)

    def b16(z):
        return z.astype(BF16)

    ones2_b16 = b16(jnp.concatenate([ones_bd, ones_bd], axis=0))
    zeros_sq = jnp.zeros((RWKV_PAIR, RWKV_PAIR), F32)

    n2 = RWKV_PAIR
    nch = tb // c
    rowc = lax.broadcasted_iota(jnp.int32, (tb, n2), 0) % c

    prep = []
    for p in range(npp):
        ln = slice(p * n2, (p + 1) * n2)
        r = r_ref[0, :, ln]
        k = k_ref[0, :, ln]
        v = v_ref[0, :, ln]
        z = w0_ref[:, ln] + wpre_ref[:, ln]
        w_log = jnp.minimum(z, 0.0) - jnp.log(1.0 + jnp.exp(-jnp.abs(z))) - 0.5
        lw = -jnp.exp(w_log)
        a = jax.nn.sigmoid(a0_ref[:, ln] + apre_ref[:, ln])
        kk = k * kk_ref[:, ln]
        kk = kk / jnp.maximum(jnp.sqrt(head_sum(kk * kk)), 1e-12)
        k2 = k * (1.0 + (a - 1.0) * ka_ref[:, ln])
        bv = kk * a
        cl = lw
        d = 1
        while d < c:
            cl = cl + jnp.where(rowc >= d, pltpu.roll(cl, d, axis=0), 0.0)
            d *= 2
        prep.append((r, k2, v, kk, bv, lw, cl))

    items = [(p, ch) for p in range(npp) for ch in range(nch)]
    rt, at16, v16, lhs16, diag, a_ab, a_ak16 = {}, {}, {}, {}, {}, {}, {}
    for it in items:
        p, ch = it
        r, k2, v, kk, bv, lw, cl = prep[p]
        sl = slice(ch * c, (ch + 1) * c)
        clc = cl[sl]
        cle = clc[c - 1 : c, :]
        e_neg = jnp.exp(-clc)
        e_end = jnp.exp(cle - clc)
        rt_s = stack(r[sl] * jnp.exp(clc))
        at_s = stack(-kk[sl] * jnp.exp(clc - lw[sl]))
        gmat = _dot_nt(b16(jnp.concatenate([at_s, rt_s], axis=0)),
                       b16(jnp.concatenate([stack(bv[sl] * e_neg), stack(k2[sl] * e_neg)], axis=0)))
        a_ab[it] = gmat[:n2, :n2] * strict
        a_ak16[it] = b16(gmat[:n2, n2:] * strict)
        ends_t = jnp.concatenate([stack(bv[sl] * e_end).T, stack(k2[sl] * e_end).T], axis=1)
        lhs16[it] = b16(jnp.concatenate([gmat[n2:] * jnp.concatenate([incl, incl], axis=1), ends_t], axis=0))
        rt[it] = rt_s
        at16[it] = b16(at_s)
        v16[it] = b16(stack(v[sl]))
        diag[it] = eye * jnp.exp(cle)

    av16 = {it: b16(_dot(a_ak16[it], v16[it])) for it in items}
    tinv = {it: eye + a_ab[it] for it in items}
    l16 = {it: b16(a_ab[it]) for it in items}
    pw = {it: _dot(l16[it], l16[it]) for it in items}
    span = 2
    while span < c:
        span *= 2
        for it in items:
            p16 = b16(pw[it])
            if span < c:
                res = _dot(p16, jnp.concatenate([b16(tinv[it]), p16], axis=1))
                tinv[it] = tinv[it] + res[:, :n2]
                pw[it] = res[:, n2:]
            else:
                tinv[it] = tinv[it] + _dot(p16, b16(tinv[it]))

    big = {}
    for it in items:
        wu16 = b16(_dot(b16(tinv[it]), jnp.concatenate([at16[it], av16[it]], axis=1)))
        rhs16 = jnp.concatenate([wu16, jnp.concatenate([b16(zeros_sq), v16[it]], axis=1)], axis=0)
        big[it] = _dot(lhs16[it], rhs16)

    st = {p: st_ref[p] for p in range(npp)}
    st_in = {}
    for ch in range(nch):
        for p in range(npp):
            it = (p, ch)
            st_in[it] = st[p]
            st[p] = _dot_bf16x3(diag[it] + big[it][n2:, :n2], st[p]) + big[it][n2:, n2:]
    for p in range(npp):
        st_ref[p] = st[p]

    for p in range(npp):
        ln = slice(p * n2, (p + 1) * n2)
        r, k2, v, kk, bv, lw, cl = prep[p]
        ys = []
        for ch in range(nch):
            it = (p, ch)
            y_s = _dot(b16(rt[it] + big[it][:n2, :n2]), b16(st_in[it])) + big[it][:n2, n2:]
            ys.append(y_s[:c] + y_s[c:])
        y = jnp.concatenate(ys, axis=0)
        mean = head_sum(y) * (1.0 / hd)
        yc = y - mean
        var = head_sum(yc * yc) * (1.0 / hd)
        yn = yc * lax.rsqrt(var + RWKV_LNX_EPS) * lg_ref[:, ln] + lb_ref[:, ln]
        bonus = head_sum(r * k2 * rk_ref[:, ln]) * v
        o_ref[:, ln] = ((yn + bonus) * g_ref[:, ln]).astype(o_ref.dtype)


def _rwkv_core(rkv, w_pre, a_pre, g, w0, a0, k_k, k_a, r_k, lnx_g, lnx_b, bsz, seqlen, tb=1024, npp=2):
    _, n, d = rkv.shape
    tb = min(tb, seqlen)
    nt = seqlen // tb
    wl = npp * RWKV_PAIR
    rkv_spec = lambda j: pl.BlockSpec((1, tb, wl), lambda b, p, t: (j, b * nt + t, p))
    act_spec = pl.BlockSpec((tb, wl), lambda b, p, t: (b * nt + t, p))
    row_spec = pl.BlockSpec((1, wl), lambda b, p, t: (0, p))
    return pl.pallas_call(
        functools.partial(_rwkv_core_kernel, tb=tb, npp=npp),
        grid=(bsz, d // wl, nt),
        in_specs=[rkv_spec(0), rkv_spec(1), rkv_spec(2), act_spec, act_spec, act_spec] + [row_spec] * 7,
        out_specs=act_spec,
        out_shape=jax.ShapeDtypeStruct((n, d), BF16),
        scratch_shapes=[pltpu.VMEM((npp, RWKV_PAIR, RWKV_PAIR), F32)],
        compiler_params=_params(3),
        name="rwkv_core",
    )(rkv, rkv, rkv, w_pre, a_pre, g, w0, a0, k_k, k_a, r_k, lnx_g, lnx_b)


def _pool_kernel(x_ref, prev_ref, w_ref, sc_ref, g_ref, b_ref, o_ref, *, tm, seqlen, alpha):
    start = (pl.program_id(0) * tm) % seqlen
    x = x_ref[...]
    prev = jnp.where(start == 0, 0.0, prev_ref[...])
    xe = jnp.concatenate([prev, x], axis=0)
    pos = start + lax.broadcasted_iota(jnp.int32, (tm, 1), 0)
    gw = w_ref.shape[1]
    ys = []
    for gi, win in enumerate(POOL_WINDOWS):
        sl = slice(gi * gw, (gi + 1) * gw)
        s = xe[:, sl]
        d = 1
        while d < win:
            s = s + pltpu.roll(s, d, axis=0)
            d *= 2
        cnt = jnp.minimum(pos + 1, win).astype(F32)
        pooled = s[POOL_HALO:] / cnt - x[:, sl]
        ys.append(_dot(pooled.astype(BF16), w_ref[gi]))
    y = jnp.concatenate(ys, axis=1) * sc_ref[...]
    o_ref[...] = _layer_norm_rows(alpha * x + y, g_ref[...], b_ref[...])


def _pool_ln(x, w_pool, scale, g, b, seqlen, alpha, tm=512):
    n, d = x.shape
    tm = min(tm, seqlen)
    step = tm // POOL_HALO
    row_spec = pl.BlockSpec((1, d), lambda i: (0, 0))
    return pl.pallas_call(
        functools.partial(_pool_kernel, tm=tm, seqlen=seqlen, alpha=alpha),
        grid=(n // tm,),
        in_specs=[
            pl.BlockSpec((tm, d), lambda i: (i, 0)),
            pl.BlockSpec((POOL_HALO, d), lambda i: (jnp.maximum(i * step - 1, 0), 0)),
            pl.BlockSpec(w_pool.shape, lambda i: (0, 0, 0)),
            row_spec,
            row_spec,
            row_spec,
        ],
        out_specs=pl.BlockSpec((tm, d), lambda i: (i, 0)),
        out_shape=jax.ShapeDtypeStruct((n, d), F32),
        compiler_params=_params(1),
        name="pool_ln",
    )(x, x, w_pool, scale, g, b)


def kernel(x, ln_g, ln_b, mlp_w1, mlp_w2, rg_w_in, rg_conv_w, rg_conv_b, rg_gate_a_w, rg_gate_a_b, rg_gate_x_w, rg_gate_x_b, rg_lambda, rg_w_out, moba_w_qkv, moba_w_out, rwkv_mu, rwkv_w_rkv, rwkv_w0, rwkv_w_w1, rwkv_w_w2, rwkv_a0, rwkv_a_w1, rwkv_a_w2, rwkv_g_w1, rwkv_g_w2, rwkv_k_k, rwkv_k_a, rwkv_r_k, rwkv_lnx_g, rwkv_lnx_b, rwkv_w_out, pool_w, pool_scale):
    bsz, seqlen, d = x.shape
    depth = ln_g.shape[0]
    n_mixers = 4
    alpha = (2.0 * depth) ** 0.25
    bf = lambda w: w.astype(BF16)
    row = lambda p: p.reshape(1, -1)

    h = x.reshape(bsz * seqlen, d)
    counts = [0] * n_mixers
    for layer in range(depth):
        m = layer % n_mixers
        j = counts[m]
        counts[m] += 1
        g0, b0 = row(ln_g[layer, 0]), row(ln_b[layer, 0])
        if m == 0:
            proj = _matmul(h, bf(rg_w_in[j]), F32)
            mixed = _rglru(proj, rg_conv_w[j], row(rg_conv_b[j]), bf(rg_gate_a_w[j]), row(rg_gate_a_b[j]),
                           bf(rg_gate_x_w[j]), row(rg_gate_x_b[j]), row(rg_lambda[j]), bsz, seqlen)
            h = _matmul_ln(mixed, bf(rg_w_out[j]), h, g0, b0, alpha)
        elif m == 1:
            qkv = _qkv_rope(h, bf(moba_w_qkv[j]), seqlen)
            att = _moba_attention(qkv, bsz, seqlen)
            h = _matmul_ln(att, bf(moba_w_out[j]), h, g0, b0, alpha)
        elif m == 2:
            rkv = _rwkv_proj(h, rwkv_mu[j], bf(rwkv_w_rkv[j]), seqlen)
            w_pre, a_pre, gate = _rwkv_lora(h, rwkv_mu[j], bf(rwkv_w_w1[j]), bf(rwkv_w_w2[j]), bf(rwkv_a_w1[j]),
                                            bf(rwkv_a_w2[j]), bf(rwkv_g_w1[j]), bf(rwkv_g_w2[j]), seqlen)
            mixed = _rwkv_core(rkv, w_pre, a_pre, gate, row(rwkv_w0[j]), row(rwkv_a0[j]), row(rwkv_k_k[j]), row(rwkv_k_a[j]),
                               row(rwkv_r_k[j]), row(rwkv_lnx_g[j]), row(rwkv_lnx_b[j]), bsz, seqlen)
            h = _matmul_ln(mixed, bf(rwkv_w_out[j]), h, g0, b0, alpha)
        else:
            h = _pool_ln(h, bf(pool_w[j]), row(pool_scale[j]), g0, b0, seqlen, alpha)
        h = _mlp_ln(h, bf(mlp_w1[layer]), bf(mlp_w2[layer]), row(ln_g[layer, 1]), row(ln_b[layer, 1]), alpha)
    return h.reshape(bsz, seqlen, d)
```

```python
import functools

import jax
import jax.numpy as jnp
from jax import lax
from jax.experimental import pallas as pl
from jax.experimental.pallas import tpu as pltpu

F32 = jnp.float32
BF16 = jnp.bfloat16
HIGHEST = lax.Precision.HIGHEST

LN_EPS = 1e-5
NEG_INF = -1e30
MOBA_TAKEN = -3e38
VMEM_LIMIT_BYTES = 48 * 1024 * 1024
MLP_VMEM_LIMIT_BYTES = 58 * 1024 * 1024

LRU_C = 8.0
CONV_WIDTH = 4
ATTN_HEAD_DIM = 128
MOBA_BLOCK = 256
MOBA_TOPK = 3
ROPE_THETA = 500000.0
ROPE_DIM = ATTN_HEAD_DIM // 4
RWKV_HEAD_DIM = 64
RWKV_PAIR = 2 * RWKV_HEAD_DIM
RWKV_CHUNK = 64
RWKV_LNX_EPS = 64e-5
POOL_WINDOWS = (2, 4, 8, 16)
POOL_HALO = 16
LANES = 128
SUBLANES = 8


def _params(n_axes, vmem_limit_bytes=VMEM_LIMIT_BYTES):
    return pltpu.CompilerParams(dimension_semantics=("arbitrary",) * n_axes, vmem_limit_bytes=vmem_limit_bytes)


def _layer_norm_rows(z, g, b):
    mean = jnp.mean(z, axis=-1, keepdims=True)
    zc = z - mean
    var = jnp.mean(zc * zc, axis=-1, keepdims=True)
    return zc * lax.rsqrt(var + LN_EPS) * g + b


def _softplus(z):
    return jnp.maximum(z, 0.0) + jnp.log1p(jnp.exp(-jnp.abs(z)))


def _dot(a, b, precision=None):
    return jnp.dot(a, b, preferred_element_type=F32, precision=precision)


def _split_bf16(z):
    hi = z.astype(BF16)
    return hi, (z - hi.astype(F32)).astype(BF16)


def _dot_bf16x3(a, b):
    a_hi, a_lo = _split_bf16(a)
    b_hi, b_lo = _split_bf16(b)
    return _dot(jnp.concatenate([a_hi, a_lo], axis=1), jnp.concatenate([b_hi, b_hi], axis=0)) + _dot(a_hi, b_lo)


def _dot_nt(a, b, precision=None):
    return lax.dot_general(a, b, (((1,), (1,)), ((), ())), preferred_element_type=F32, precision=precision)


def _mm_kernel(a_ref, w_ref, o_ref):
    o_ref[...] = _dot(a_ref[...].astype(BF16), w_ref[...]).astype(o_ref.dtype)


def _matmul(a, w, out_dtype, tm=1024, tn=1024):
    n, k = a.shape
    nout = w.shape[1]
    tn = min(tn, nout)
    return pl.pallas_call(
        _mm_kernel,
        grid=(n // tm, nout // tn),
        in_specs=[pl.BlockSpec((tm, k), lambda i, j: (i, 0)), pl.BlockSpec((k, tn), lambda i, j: (0, j))],
        out_specs=pl.BlockSpec((tm, tn), lambda i, j: (i, j)),
        out_shape=jax.ShapeDtypeStruct((n, nout), out_dtype),
        compiler_params=_params(2),
        name="matmul",
    )(a, w)


def _mm_ln_kernel(a_ref, w_ref, x_ref, g_ref, b_ref, o_ref, acc_ref, *, alpha):
    @pl.when(pl.program_id(0) == 0)
    def _():
        acc_ref[...] = jnp.zeros_like(acc_ref)

    o_ref[...] = _layer_norm_rows(alpha * x_ref[...] + acc_ref[...], g_ref[...], b_ref[...])
    acc_ref[...] = _dot(a_ref[...].astype(BF16), w_ref[...])


def _matmul_ln(a, w, x, g, b, alpha, tm=512):
    n, k = a.shape
    d = w.shape[1]
    ni = n // tm
    cur = lambda s: jnp.minimum(s, ni - 1)
    lag = lambda s: jnp.maximum(s - 1, 0)
    return pl.pallas_call(
        functools.partial(_mm_ln_kernel, alpha=alpha),
        grid=(ni + 1,),
        in_specs=[
            pl.BlockSpec((tm, k), lambda s: (cur(s), 0)),
            pl.BlockSpec((k, d), lambda s: (0, 0)),
            pl.BlockSpec((tm, d), lambda s: (lag(s), 0)),
            pl.BlockSpec((1, d), lambda s: (0, 0)),
            pl.BlockSpec((1, d), lambda s: (0, 0)),
        ],
        out_specs=pl.BlockSpec((tm, d), lambda s: (lag(s), 0)),
        out_shape=jax.ShapeDtypeStruct((n, d), F32),
        scratch_shapes=[pltpu.VMEM((tm, d), F32)],
        compiler_params=_params(1),
        name="matmul_ln",
    )(a, w, x, g, b)


def _mlp_kernel(x_ref, w1_ref, w2_ref, g_ref, b_ref, o_ref, xb_ref, *, alpha, nf):
    f = pl.program_id(1)

    def partial_out(xb):
        h = jnp.maximum(_dot(xb, w1_ref[0]), 0.0)
        return _dot((h * h).astype(BF16), w2_ref[0])

    def first():
        xb = x_ref[...].astype(BF16)
        xb_ref[...] = xb
        return alpha * x_ref[...] + partial_out(xb)

    if nf == 1:
        o_ref[...] = _layer_norm_rows(first(), g_ref[...], b_ref[...])
        return

    @pl.when(f == 0)
    def _():
        o_ref[...] = first()

    @pl.when(jnp.logical_and(f > 0, f < nf - 1))
    def _():
        o_ref[...] += partial_out(xb_ref[...])

    @pl.when(f == nf - 1)
    def _():
        o_ref[...] = _layer_norm_rows(o_ref[...] + partial_out(xb_ref[...]), g_ref[...], b_ref[...])


def _mlp_ln(x, w1, w2, layer, g, b, alpha, tm=512, tf=2048):
    n, d = x.shape
    dff = w1.shape[2]
    tf = min(tf, dff)
    return pl.pallas_call(
        functools.partial(_mlp_kernel, alpha=alpha, nf=dff // tf),
        grid=(n // tm, dff // tf),
        in_specs=[
            pl.BlockSpec((tm, d), lambda i, f: (i, 0)),
            pl.BlockSpec((1, d, tf), lambda i, f: (layer, 0, f)),
            pl.BlockSpec((1, tf, d), lambda i, f: (layer, f, 0)),
            pl.BlockSpec((1, d), lambda i, f: (0, 0)),
            pl.BlockSpec((1, d), lambda i, f: (0, 0)),
        ],
        out_specs=pl.BlockSpec((tm, d), lambda i, f: (i, 0)),
        out_shape=jax.ShapeDtypeStruct((n, d), F32),
        scratch_shapes=[pltpu.VMEM((tm, d), BF16)],
        compiler_params=_params(2, MLP_VMEM_LIMIT_BYTES),
        name="mlp_ln",
    )(x, w1, w2, g, b)


def _rglru_kernel(gate_ref, u_ref, cw_ref, cb_ref, aw_ref, ab_ref, xw_ref, xb_ref, lam_ref, o_ref, hc_ref, ut_ref, *, ts):
    @pl.when(pl.program_id(2) == 0)
    def _():
        hc_ref[...] = jnp.zeros_like(hc_ref)
        ut_ref[...] = jnp.zeros_like(ut_ref)

    u = u_ref[...]
    ue = jnp.concatenate([ut_ref[...], u], axis=0)
    conv = cw_ref[CONV_WIDTH - 1 : CONV_WIDTH, :] * u + cb_ref[...]
    for d in range(1, CONV_WIDTH):
        conv = conv + cw_ref[CONV_WIDTH - 1 - d : CONV_WIDTH - d, :] * pltpu.roll(ue, d, axis=0)[SUBLANES:]
    ut_ref[...] = u[ts - SUBLANES :, :]

    cb16 = conv.astype(BF16)
    r = jax.nn.sigmoid(_dot(cb16, aw_ref[0]) + ab_ref[...])
    i = jax.nn.sigmoid(_dot(cb16, xw_ref[0]) + xb_ref[...])
    log_a = (-LRU_C) * r * _softplus(-lam_ref[...])
    a = jnp.exp(log_a)
    b = conv * i * jnp.sqrt(1.0 - a * a)

    w = a.shape[1]
    a = a.reshape(ts // SUBLANES, SUBLANES, w)
    b = b.reshape(ts // SUBLANES, SUBLANES, w)
    row = lax.broadcasted_iota(jnp.int32, a.shape, 1)
    d = 1
    while d < SUBLANES:
        keep = row >= d
        a_sh = jnp.where(keep, pltpu.roll(a, d, axis=1), 1.0)
        b_sh = jnp.where(keep, pltpu.roll(b, d, axis=1), 0.0)
        b = a * b_sh + b
        a = a * a_sh
        d *= 2
    h_prev = hc_ref[0:1, :]
    hs = []
    for g in range(ts // SUBLANES):
        hg = a[g] * h_prev + b[g]
        hs.append(hg)
        h_prev = hg[SUBLANES - 1 : SUBLANES, :]
    h = jnp.concatenate(hs, axis=0)
    hc_ref[...] = jnp.broadcast_to(h_prev, hc_ref.shape)
    o_ref[...] = (jax.nn.gelu(gate_ref[...]) * h).astype(o_ref.dtype)


def _rglru(proj, conv_w, conv_b, gate_a_w, gate_a_b, gate_x_w, gate_x_b, lam, bsz, seqlen, ts=1024):
    n, r2 = proj.shape
    r = r2 // 2
    nblk, w, _ = gate_a_w.shape
    ts = min(ts, seqlen)
    nt = seqlen // ts
    row_spec = pl.BlockSpec((1, w), lambda b, j, t: (0, j))
    gw_spec = pl.BlockSpec((1, w, w), lambda b, j, t: (j, 0, 0))
    return pl.pallas_call(
        functools.partial(_rglru_kernel, ts=ts),
        grid=(bsz, nblk, nt),
        in_specs=[
            pl.BlockSpec((ts, w), lambda b, j, t: (b * nt + t, j)),
            pl.BlockSpec((ts, w), lambda b, j, t: (b * nt + t, nblk + j)),
            pl.BlockSpec((CONV_WIDTH, w), lambda b, j, t: (0, j)),
            row_spec,
            gw_spec,
            row_spec,
            gw_spec,
            row_spec,
            row_spec,
        ],
        out_specs=pl.BlockSpec((ts, w), lambda b, j, t: (b * nt + t, j)),
        out_shape=jax.ShapeDtypeStruct((n, r), BF16),
        scratch_shapes=[pltpu.VMEM((SUBLANES, w), F32), pltpu.VMEM((SUBLANES, w), F32)],
        compiler_params=_params(3),
        name="rglru",
    )(proj, proj, conv_w, conv_b, gate_a_w, gate_a_b, gate_x_w, gate_x_b, lam)


def _qkv_kernel(x_ref, w_ref, c_ref, s1_ref, s2_ref, o_ref, acc_ref, xb_ref, *, tn, nj):
    @pl.when(pl.program_id(0) == 0)
    def _():
        acc_ref[...] = jnp.zeros_like(acc_ref)

    @pl.when(pl.program_id(0) % nj == 0)
    def _():
        xb_ref[...] = x_ref[...].astype(BF16)

    half = ROPE_DIM // 2
    rep = tn // ATTN_HEAD_DIM
    prev = acc_ref[...]
    c = jnp.concatenate([c_ref[0]] * rep, axis=1)
    s1 = jnp.concatenate([s1_ref[0]] * rep, axis=1)
    s2 = jnp.concatenate([s2_ref[0]] * rep, axis=1)
    roped = prev * c + pltpu.roll(prev, half, axis=1) * s1 + pltpu.roll(prev, tn - half, axis=1) * s2
    o_ref[...] = roped.astype(o_ref.dtype)
    acc_ref[...] = _dot(xb_ref[...], w_ref[...])


def _rope_tables(seqlen, scale):
    half = ROPE_DIM // 2
    pos = jnp.arange(seqlen, dtype=F32)
    inv_freq = ROPE_THETA ** (-jnp.arange(0, ROPE_DIM, 2, dtype=F32) / ROPE_DIM)
    ang = pos[:, None] * inv_freq[None, :]
    cos, sin = jnp.cos(ang), jnp.sin(ang)
    pad = ATTN_HEAD_DIM - ROPE_DIM
    c = jnp.concatenate([cos, cos, jnp.ones((seqlen, pad), F32)], axis=1)
    s1 = jnp.concatenate([jnp.zeros((seqlen, half), F32), sin, jnp.zeros((seqlen, pad), F32)], axis=1)
    s2 = jnp.concatenate([-sin, jnp.zeros((seqlen, half + pad), F32)], axis=1)
    ident = jnp.ones_like(c)
    zero = jnp.zeros_like(c)
    return (jnp.stack([c * scale, c, ident]), jnp.stack([s1 * scale, s1, zero]), jnp.stack([s2 * scale, s2, zero]))


def _qkv_rope(x, w_qkv, seqlen, tm=512, tn=1024):
    n, d = x.shape
    nout = w_qkv.shape[1]
    tm = min(tm, seqlen)
    tn = min(tn, d)
    c, s1, s2 = _rope_tables(seqlen, ATTN_HEAD_DIM**-0.5)
    nt = seqlen // tm
    ni, nj = n // tm, nout // tn
    last = ni * nj - 1
    cur = lambda s: jnp.minimum(s, last)
    lag = lambda s: jnp.maximum(s - 1, 0)
    tab_spec = pl.BlockSpec((1, tm, ATTN_HEAD_DIM), lambda s: ((lag(s) % nj) * tn // d, (lag(s) // nj) % nt, 0))
    return pl.pallas_call(
        functools.partial(_qkv_kernel, tn=tn, nj=nj),
        grid=(ni * nj + 1,),
        in_specs=[
            pl.BlockSpec((tm, d), lambda s: (cur(s) // nj, 0)),
            pl.BlockSpec((d, tn), lambda s: (0, cur(s) % nj)),
            tab_spec,
            tab_spec,
            tab_spec,
        ],
        out_specs=pl.BlockSpec((tm, tn), lambda s: (lag(s) // nj, lag(s) % nj)),
        out_shape=jax.ShapeDtypeStruct((n, nout), BF16),
        scratch_shapes=[pltpu.VMEM((tm, tn), F32), pltpu.VMEM((tm, d), BF16)],
        compiler_params=_params(1),
        name="qkv_rope",
    )(x, w_qkv, c, s1, s2)


def _moba_kernel(q_ref, k_ref, v_ref, o_ref, kmean_ref, vt_ref, sel_ref, s_ref, *, nblk, nh, nq):
    t = pl.program_id(2)
    blk, hd = MOBA_BLOCK, ATTN_HEAD_DIM

    heads = range(nh)
    tiles = range(nq)
    combos = [(a, h) for a in tiles for h in heads]
    lanes = [slice(h * hd, (h + 1) * hd) for h in heads]
    rows = [slice(a * blk, (a + 1) * blk) for a in tiles]
    own = [nq * t + a for a in tiles]

    @pl.when(t == 0)
    def _():
        for h in heads:
            kf = k_ref[:, lanes[h]].astype(F32).reshape(nblk, blk, hd)
            kmean_ref[h] = jnp.sum(kf, axis=1) * (1.0 / blk)
            vt_ref[h] = v_ref[:, lanes[h]].astype(F32).T.astype(BF16)

    q = {(a, h): q_ref[rows[a], lanes[h]] for a, h in combos}

    def scores(jp, a, h):
        off = pl.multiple_of(jp * (2 * blk), 2 * blk)
        return _dot_nt(k_ref[pl.ds(off, 2 * blk), lanes[h]], q[a, h])

    for a, h in combos:
        s_ref[a, 0, h] = scores(0, a, h)

    for a, h in combos:
        gate = _dot_nt(kmean_ref[h], q[a, h].astype(F32), precision=HIGHEST)
        blk_id = lax.broadcasted_iota(jnp.int32, gate.shape, 0).astype(F32)
        past = blk_id < own[a].astype(F32)
        g = jnp.where(past, gate, NEG_INF)
        sel = jnp.zeros_like(gate)
        for _ in range(MOBA_TOPK):
            top = jnp.max(g, axis=0, keepdims=True)
            first = jnp.min(jnp.where(g == top, blk_id, float(nblk)), axis=0, keepdims=True)
            hit = blk_id == first
            sel = jnp.where(hit, 1.0, sel)
            g = jnp.where(hit, MOBA_TAKEN, g)
        sel_ref[a, h] = jnp.where(past, sel, 0.0)

    key_pos = lax.broadcasted_iota(jnp.int32, (blk, blk), 0)
    qry_pos = lax.broadcasted_iota(jnp.int32, (blk, blk), 1)
    carry = {}
    for a, h in combos:
        off_own = pl.multiple_of(own[a] * blk, blk)
        s = _dot_nt(k_ref[pl.ds(off_own, blk), lanes[h]], q[a, h])
        s = jnp.where(key_pos <= qry_pos, s, NEG_INF)
        m = jnp.max(s, axis=0, keepdims=True)
        p = jnp.exp(s - m)
        l = jnp.sum(p, axis=0, keepdims=True)
        acc = _dot(vt_ref[h, :, pl.ds(off_own, blk)], p.astype(BF16))
        carry[a, h] = (m, l, acc)

    def make_body(a):
        def body(jp, carry):
            off = pl.multiple_of(jp * (2 * blk), 2 * blk)
            s_cur = [s_ref[a, jp % 2, h] for h in heads]
            jn = jnp.minimum(jp + 1, nblk // 2 - 1)
            for h in heads:
                s_ref[a, (jp + 1) % 2, h] = scores(jn, a, h)
            out = []
            for h in heads:
                m, l, acc = carry[h]
                s0 = jnp.where(sel_ref[a, h, pl.ds(2 * jp, 1), :] > 0.5, s_cur[h][:blk], NEG_INF)
                s1 = jnp.where(sel_ref[a, h, pl.ds(2 * jp + 1, 1), :] > 0.5, s_cur[h][blk:], NEG_INF)
                m_blk = jnp.maximum(jnp.max(s0, axis=0, keepdims=True), jnp.max(s1, axis=0, keepdims=True))
                m_new = jnp.maximum(m, m_blk)
                alpha = jnp.exp(m - m_new)
                p0 = jnp.exp(s0 - m_new)
                p1 = jnp.exp(s1 - m_new)
                l = alpha * l + jnp.sum(p0, axis=0, keepdims=True) + jnp.sum(p1, axis=0, keepdims=True)
                p = jnp.concatenate([p0.astype(BF16), p1.astype(BF16)], axis=0)
                acc = acc * alpha + _dot(vt_ref[h, :, pl.ds(off, 2 * blk)], p)
                out.append((m_new, l, acc))
            return tuple(out)

        return body

    done = {}
    for a in tiles:
        res = lax.fori_loop(0, (own[a] + 1) // 2, make_body(a), tuple(carry[a, h] for h in heads))
        for h in heads:
            done[a, h] = res[h]
    for a, h in combos:
        m, l, acc = done[a, h]
        o_ref[rows[a], lanes[h]] = (acc / l).T.astype(o_ref.dtype)


def _moba_attention(qkv, bsz, seqlen, nh=2, nq=4):
    n, d3 = qkv.shape
    d = d3 // 3
    ngrp = d // (nh * ATTN_HEAD_DIM)
    nblk = seqlen // MOBA_BLOCK
    assert nblk % 2 == 0 and nblk % nq == 0
    blk, hd = MOBA_BLOCK, ATTN_HEAD_DIM
    wl = nh * hd
    nt = nblk // nq
    return pl.pallas_call(
        functools.partial(_moba_kernel, nblk=nblk, nh=nh, nq=nq),
        grid=(bsz, ngrp, nt),
        in_specs=[
            pl.BlockSpec((nq * blk, wl), lambda b, h, t: (b * nt + t, h)),
            pl.BlockSpec((seqlen, wl), lambda b, h, t: (b, ngrp + h)),
            pl.BlockSpec((seqlen, wl), lambda b, h, t: (b, 2 * ngrp + h)),
        ],
        out_specs=pl.BlockSpec((nq * blk, wl), lambda b, h, t: (b * nt + t, h)),
        out_shape=jax.ShapeDtypeStruct((n, d), BF16),
        scratch_shapes=[pltpu.VMEM((nh, nblk, hd), F32), pltpu.VMEM((nh, hd, seqlen), BF16),
                        pltpu.VMEM((nq, nh, nblk, blk), F32), pltpu.VMEM((nq, 2, nh, 2 * blk, blk), F32)],
        compiler_params=_params(3),
        name="moba_attention",
    )(qkv, qkv, qkv)


def _token_shift_delta(x, prev_ref, is_start):
    prev_row = jnp.where(is_start, 0.0, prev_ref[SUBLANES - 1 : SUBLANES, :])
    row = lax.broadcasted_iota(jnp.int32, x.shape, 0)
    return jnp.where(row == 0, prev_row, pltpu.roll(x, 1, axis=0)) - x


def _rwkv_proj_kernel(x_ref, prev_ref, mu_ref, w_ref, o_ref, *, tm, seqlen):
    x = x_ref[...]
    xx = _token_shift_delta(x, prev_ref, (pl.program_id(1) * tm) % seqlen == 0)
    o_ref[0] = _dot((x + xx * mu_ref[0]).astype(BF16), w_ref[0])


def _rwkv_proj(x, mu, w_rkv, seqlen, tm=512):
    n, d = x.shape
    tm = min(tm, seqlen)
    step = tm // SUBLANES
    return pl.pallas_call(
        functools.partial(_rwkv_proj_kernel, tm=tm, seqlen=seqlen),
        grid=(3, n // tm),
        in_specs=[
            pl.BlockSpec((tm, d), lambda g, i: (i, 0)),
            pl.BlockSpec((SUBLANES, d), lambda g, i: (jnp.maximum(i * step - 1, 0), 0)),
            pl.BlockSpec((1, 1, d), lambda g, i: (g, 0, 0)),
            pl.BlockSpec((1, d, d), lambda g, i: (g, 0, 0)),
        ],
        out_specs=pl.BlockSpec((1, tm, d), lambda g, i: (g, i, 0)),
        out_shape=jax.ShapeDtypeStruct((3, n, d), F32),
        compiler_params=_params(2),
        name="rwkv_proj",
    )(x, x, mu.reshape(mu.shape[0], 1, d), w_rkv)


def _rwkv_lora_kernel(x_ref, prev_ref, mu_ref, ww1_ref, ww2_ref, aw1_ref, aw2_ref, gw1_ref, gw2_ref, wo_ref, ao_ref, go_ref, *, tm, seqlen):
    x = x_ref[...]
    xx = _token_shift_delta(x, prev_ref, (pl.program_id(0) * tm) % seqlen == 0)
    xw = (x + xx * mu_ref[3:4, :]).astype(BF16)
    xa = (x + xx * mu_ref[4:5, :]).astype(BF16)
    xg = (x + xx * mu_ref[5:6, :]).astype(BF16)
    wo_ref[...] = _dot(jnp.tanh(_dot(xw, ww1_ref[...])).astype(BF16), ww2_ref[...])
    ao_ref[...] = _dot(_dot(xa, aw1_ref[...]).astype(BF16), aw2_ref[...])
    go_ref[...] = _dot(jax.nn.sigmoid(_dot(xg, gw1_ref[...])).astype(BF16), gw2_ref[...])


def _pad_lora(w1, w2):
    rank = w1.shape[1]
    pad = (-rank) % LANES
    return jnp.pad(w1, ((0, 0), (0, pad))), jnp.pad(w2, ((0, pad), (0, 0)))


def _rwkv_lora(x, mu, w_w1, w_w2, a_w1, a_w2, g_w1, g_w2, seqlen, tm=512):
    n, d = x.shape
    tm = min(tm, seqlen)
    step = tm // SUBLANES
    ws = [*_pad_lora(w_w1, w_w2), *_pad_lora(a_w1, a_w2), *_pad_lora(g_w1, g_w2)]
    full = lambda a: pl.BlockSpec(a.shape, lambda i: (0, 0))
    out_spec = pl.BlockSpec((tm, d), lambda i: (i, 0))
    return pl.pallas_call(
        functools.partial(_rwkv_lora_kernel, tm=tm, seqlen=seqlen),
        grid=(n // tm,),
        in_specs=[
            pl.BlockSpec((tm, d), lambda i: (i, 0)),
            pl.BlockSpec((SUBLANES, d), lambda i: (jnp.maximum(i * step - 1, 0), 0)),
            full(mu),
            *[full(w) for w in ws],
        ],
        out_specs=[out_spec, out_spec, out_spec],
        out_shape=[jax.ShapeDtypeStruct((n, d), F32)] * 3,
        compiler_params=_params(1),
        name="rwkv_lora",
    )(x, x, mu, *ws)


def _rwkv_core_kernel(r_ref, k_ref, v_ref, wpre_ref, apre_ref, g_ref, w0_ref, a0_ref, kk_ref, ka_ref, rk_ref, lg_ref, lb_ref,
                      o_ref, st_ref, *, tb, npp):
    c = RWKV_CHUNK
    hd = RWKV_HEAD_DIM

    @pl.when(pl.program_id(2) == 0)
    def _():
        st_ref[...] = jnp.zeros_like(st_ref)

    lane = lax.broadcasted_iota(jnp.int32, (1, RWKV_PAIR), 1)
    m0 = jnp.where(lane < hd, 1.0, 0.0)
    m1 = 1.0 - m0
    ri = lax.broadcasted_iota(jnp.int32, (RWKV_PAIR, RWKV_PAIR), 0)
    ci = lax.broadcasted_iota(jnp.int32, (RWKV_PAIR, RWKV_PAIR), 1)
    same = (ri < hd) == (ci < hd)
    ones_bd = jnp.where(same, 1.0, 0.0)
    strict = jnp.where(same & (ci < ri), 1.0, 0.0)
    incl = jnp.where(same & (ci <= ri), 1.0, 0.0)
    eye = jnp.where(ri == ci, 1.0, 0.0)

    def head_sum(z):
        hi, lo = _split_bf16(z)
        return _dot(jnp.concatenate([hi, lo], axis=1), ones2_b16)

    def stack(z):
        return jnp.concatenate([z * m0, z * m1], axis=0)

    def b16(z):
        return z.astype(BF16)

    ones2_b16 = b16(jnp.concatenate([ones_bd, ones_bd], axis=0))
    zeros_sq = jnp.zeros((RWKV_PAIR, RWKV_PAIR), F32)

    n2 = RWKV_PAIR
    nch = tb // c
    rowc = lax.broadcasted_iota(jnp.int32, (tb, n2), 0) % c

    prep = []
    for p in range(npp):
        ln = slice(p * n2, (p + 1) * n2)
        r = r_ref[0, :, ln]
        k = k_ref[0, :, ln]
        v = v_ref[0, :, ln]
        z = w0_ref[:, ln] + wpre_ref[:, ln]
        w_log = jnp.minimum(z, 0.0) - jnp.log(1.0 + jnp.exp(-jnp.abs(z))) - 0.5
        lw = -jnp.exp(w_log)
        a = jax.nn.sigmoid(a0_ref[:, ln] + apre_ref[:, ln])
        kk = k * kk_ref[:, ln]
        kk = kk / jnp.maximum(jnp.sqrt(head_sum(kk * kk)), 1e-12)
        k2 = k * (1.0 + (a - 1.0) * ka_ref[:, ln])
        bv = kk * a
        cl = lw
        d = 1
        while d < c:
            cl = cl + jnp.where(rowc >= d, pltpu.roll(cl, d, axis=0), 0.0)
            d *= 2
        prep.append((r, k2, v, kk, bv, lw, cl))

    items = [(p, ch) for p in range(npp) for ch in range(nch)]
    rt, at16, v16, lhs16, diag, a_ab, a_ak16 = {}, {}, {}, {}, {}, {}, {}
    for it in items:
        p, ch = it
        r, k2, v, kk, bv, lw, cl = prep[p]
        sl = slice(ch * c, (ch + 1) * c)
        clc = cl[sl]
        cle = clc[c - 1 : c, :]
        e_neg = jnp.exp(-clc)
        e_end = jnp.exp(cle - clc)
        rt_s = stack(r[sl] * jnp.exp(clc))
        at_s = stack(-kk[sl] * jnp.exp(clc - lw[sl]))
        gmat = _dot_nt(b16(jnp.concatenate([at_s, rt_s], axis=0)),
                       b16(jnp.concatenate([stack(bv[sl] * e_neg), stack(k2[sl] * e_neg)], axis=0)))
        a_ab[it] = gmat[:n2, :n2] * strict
        a_ak16[it] = b16(gmat[:n2, n2:] * strict)
        ends_t = jnp.concatenate([stack(bv[sl] * e_end).T, stack(k2[sl] * e_end).T], axis=1)
        lhs16[it] = b16(jnp.concatenate([gmat[n2:] * jnp.concatenate([incl, incl], axis=1), ends_t], axis=0))
        rt[it] = rt_s
        at16[it] = b16(at_s)
        v16[it] = b16(stack(v[sl]))
        diag[it] = eye * jnp.exp(cle)

    av16 = {it: b16(_dot(a_ak16[it], v16[it])) for it in items}
    tinv = {it: eye + a_ab[it] for it in items}
    l16 = {it: b16(a_ab[it]) for it in items}
    pw = {it: _dot(l16[it], l16[it]) for it in items}
    span = 2
    while span < c:
        span *= 2
        for it in items:
            p16 = b16(pw[it])
            if span < c:
                res = _dot(p16, jnp.concatenate([b16(tinv[it]), p16], axis=1))
                tinv[it] = tinv[it] + res[:, :n2]
                pw[it] = res[:, n2:]
            else:
                tinv[it] = tinv[it] + _dot(p16, b16(tinv[it]))

    big = {}
    for it in items:
        wu16 = b16(_dot(b16(tinv[it]), jnp.concatenate([at16[it], av16[it]], axis=1)))
        rhs16 = jnp.concatenate([wu16, jnp.concatenate([b16(zeros_sq), v16[it]], axis=1)], axis=0)
        big[it] = _dot(lhs16[it], rhs16)

    st = {p: st_ref[p] for p in range(npp)}
    st_in = {}
    for ch in range(nch):
        for p in range(npp):
            it = (p, ch)
            st_in[it] = st[p]
            st[p] = _dot_bf16x3(diag[it] + big[it][n2:, :n2], st[p]) + big[it][n2:, n2:]
    for p in range(npp):
        st_ref[p] = st[p]

    for p in range(npp):
        ln = slice(p * n2, (p + 1) * n2)
        r, k2, v, kk, bv, lw, cl = prep[p]
        ys = []
        for ch in range(nch):
            it = (p, ch)
            y_s = _dot(b16(rt[it] + big[it][:n2, :n2]), b16(st_in[it])) + big[it][:n2, n2:]
            ys.append(y_s[:c] + y_s[c:])
        y = jnp.concatenate(ys, axis=0)
        mean = head_sum(y) * (1.0 / hd)
        yc = y - mean
        var = head_sum(yc * yc) * (1.0 / hd)
        yn = yc * lax.rsqrt(var + RWKV_LNX_EPS) * lg_ref[:, ln] + lb_ref[:, ln]
        bonus = head_sum(r * k2 * rk_ref[:, ln]) * v
        o_ref[:, ln] = ((yn + bonus) * g_ref[:, ln]).astype(o_ref.dtype)


def _rwkv_core(rkv, w_pre, a_pre, g, w0, a0, k_k, k_a, r_k, lnx_g, lnx_b, bsz, seqlen, tb=512, npp=4):
    _, n, d = rkv.shape
    tb = min(tb, seqlen)
    nt = seqlen // tb
    wl = npp * RWKV_PAIR
    rkv_spec = lambda j: pl.BlockSpec((1, tb, wl), lambda b, p, t: (j, b * nt + t, p))
    act_spec = pl.BlockSpec((tb, wl), lambda b, p, t: (b * nt + t, p))
    row_spec = pl.BlockSpec((1, wl), lambda b, p, t: (0, p))
    return pl.pallas_call(
        functools.partial(_rwkv_core_kernel, tb=tb, npp=npp),
        grid=(bsz, d // wl, nt),
        in_specs=[rkv_spec(0), rkv_spec(1), rkv_spec(2), act_spec, act_spec, act_spec] + [row_spec] * 7,
        out_specs=act_spec,
        out_shape=jax.ShapeDtypeStruct((n, d), BF16),
        scratch_shapes=[pltpu.VMEM((npp, RWKV_PAIR, RWKV_PAIR), F32)],
        compiler_params=_params(3),
        name="rwkv_core",
    )(rkv, rkv, rkv, w_pre, a_pre, g, w0, a0, k_k, k_a, r_k, lnx_g, lnx_b)


def _pool_kernel(x_ref, prev_ref, w_ref, sc_ref, g_ref, b_ref, o_ref, *, tm, seqlen, alpha):
    start = (pl.program_id(0) * tm) % seqlen
    x = x_ref[...]
    prev = jnp.where(start == 0, 0.0, prev_ref[...])
    xe = jnp.concatenate([prev, x], axis=0)
    pos = start + lax.broadcasted_iota(jnp.int32, (tm, 1), 0)
    gw = w_ref.shape[1]
    ys = []
    for gi, win in enumerate(POOL_WINDOWS):
        sl = slice(gi * gw, (gi + 1) * gw)
        s = xe[:, sl]
        d = 1
        while d < win:
            s = s + pltpu.roll(s, d, axis=0)
            d *= 2
        cnt = jnp.minimum(pos + 1, win).astype(F32)
        pooled = s[POOL_HALO:] / cnt - x[:, sl]
        ys.append(_dot(pooled.astype(BF16), w_ref[gi]))
    y = jnp.concatenate(ys, axis=1) * sc_ref[...]
    o_ref[...] = _layer_norm_rows(alpha * x + y, g_ref[...], b_ref[...])


def _pool_ln(x, w_pool, scale, g, b, seqlen, alpha, tm=512):
    n, d = x.shape
    tm = min(tm, seqlen)
    step = tm // POOL_HALO
    row_spec = pl.BlockSpec((1, d), lambda i: (0, 0))
    return pl.pallas_call(
        functools.partial(_pool_kernel, tm=tm, seqlen=seqlen, alpha=alpha),
        grid=(n // tm,),
        in_specs=[
            pl.BlockSpec((tm, d), lambda i: (i, 0)),
            pl.BlockSpec((POOL_HALO, d), lambda i: (jnp.maximum(i * step - 1, 0), 0)),
            pl.BlockSpec(w_pool.shape, lambda i: (0, 0, 0)),
            row_spec,
            row_spec,
            row_spec,
        ],
        out_specs=pl.BlockSpec((tm, d), lambda i: (i, 0)),
        out_shape=jax.ShapeDtypeStruct((n, d), F32),
        compiler_params=_params(1),
        name="pool_ln",
    )(x, x, w_pool, scale, g, b)


def kernel(x, ln_g, ln_b, mlp_w1, mlp_w2, rg_w_in, rg_conv_w, rg_conv_b, rg_gate_a_w, rg_gate_a_b, rg_gate_x_w, rg_gate_x_b, rg_lambda, rg_w_out, moba_w_qkv, moba_w_out, rwkv_mu, rwkv_w_rkv, rwkv_w0, rwkv_w_w1, rwkv_w_w2, rwkv_a0, rwkv_a_w1, rwkv_a_w2, rwkv_g_w1, rwkv_g_w2, rwkv_k_k, rwkv_k_a, rwkv_r_k, rwkv_lnx_g, rwkv_lnx_b, rwkv_w_out, pool_w, pool_scale):
    bsz, seqlen, d = x.shape
    depth = ln_g.shape[0]
    n_mixers = 4
    alpha = (2.0 * depth) ** 0.25
    bf = lambda w: w.astype(BF16)
    row = lambda p: p.reshape(1, -1)
    mlp_w1_b16, mlp_w2_b16 = bf(mlp_w1), bf(mlp_w2)

    h = x.reshape(bsz * seqlen, d)
    counts = [0] * n_mixers
    for layer in range(depth):
        m = layer % n_mixers
        j = counts[m]
        counts[m] += 1
        g0, b0 = row(ln_g[layer, 0]), row(ln_b[layer, 0])
        if m == 0:
            proj = _matmul(h, bf(rg_w_in[j]), F32)
            mixed = _rglru(proj, rg_conv_w[j], row(rg_conv_b[j]), bf(rg_gate_a_w[j]), row(rg_gate_a_b[j]),
                           bf(rg_gate_x_w[j]), row(rg_gate_x_b[j]), row(rg_lambda[j]), bsz, seqlen)
            h = _matmul_ln(mixed, bf(rg_w_out[j]), h, g0, b0, alpha)
        elif m == 1:
            qkv = _qkv_rope(h, bf(moba_w_qkv[j]), seqlen)
            att = _moba_attention(qkv, bsz, seqlen)
            h = _matmul_ln(att, bf(moba_w_out[j]), h, g0, b0, alpha)
        elif m == 2:
            rkv = _rwkv_proj(h, rwkv_mu[j], bf(rwkv_w_rkv[j]), seqlen)
            w_pre, a_pre, gate = _rwkv_lora(h, rwkv_mu[j], bf(rwkv_w_w1[j]), bf(rwkv_w_w2[j]), bf(rwkv_a_w1[j]),
                                            bf(rwkv_a_w2[j]), bf(rwkv_g_w1[j]), bf(rwkv_g_w2[j]), seqlen)
            mixed = _rwkv_core(rkv, w_pre, a_pre, gate, row(rwkv_w0[j]), row(rwkv_a0[j]), row(rwkv_k_k[j]), row(rwkv_k_a[j]),
                               row(rwkv_r_k[j]), row(rwkv_lnx_g[j]), row(rwkv_lnx_b[j]), bsz, seqlen)
            h = _matmul_ln(mixed, bf(rwkv_w_out[j]), h, g0, b0, alpha)
        else:
            h = _pool_ln(h, bf(pool_w[j]), row(pool_scale[j]), g0, b0, seqlen, alpha)
        h = _mlp_ln(h, mlp_w1_b16, mlp_w2_b16, layer, row(ln_g[layer, 1]), row(ln_b[layer, 1]), alpha)
    return h.reshape(bsz, seqlen, d)
```

```python
import functools

import jax
import jax.numpy as jnp
from jax import lax
from jax.experimental import pallas as pl
from jax.experimental.pallas import tpu as pltpu

F32 = jnp.float32
BF16 = jnp.bfloat16
HIGHEST = lax.Precision.HIGHEST

LN_EPS = 1e-5
NEG_INF = -1e30
MOBA_TAKEN = -3e38
VMEM_LIMIT_BYTES = 48 * 1024 * 1024
MLP_VMEM_LIMIT_BYTES = 58 * 1024 * 1024

LRU_C = 8.0
CONV_WIDTH = 4
ATTN_HEAD_DIM = 128
MOBA_BLOCK = 256
MOBA_TOPK = 3
ROPE_THETA = 500000.0
ROPE_DIM = ATTN_HEAD_DIM // 4
RWKV_HEAD_DIM = 64
RWKV_PAIR = 2 * RWKV_HEAD_DIM
RWKV_CHUNK = 64
RWKV_LNX_EPS = 64e-5
POOL_WINDOWS = (2, 4, 8, 16)
POOL_HALO = 16
LANES = 128
SUBLANES = 8


def _params(n_axes, vmem_limit_bytes=VMEM_LIMIT_BYTES):
    return pltpu.CompilerParams(dimension_semantics=("arbitrary",) * n_axes, vmem_limit_bytes=vmem_limit_bytes)


def _layer_norm_rows(z, g, b):
    mean = jnp.mean(z, axis=-1, keepdims=True)
    zc = z - mean
    var = jnp.mean(zc * zc, axis=-1, keepdims=True)
    return zc * lax.rsqrt(var + LN_EPS) * g + b


def _softplus(z):
    return jnp.maximum(z, 0.0) + jnp.log1p(jnp.exp(-jnp.abs(z)))


def _dot(a, b, precision=None):
    return jnp.dot(a, b, preferred_element_type=F32, precision=precision)


def _split_bf16(z):
    hi = z.astype(BF16)
    return hi, (z - hi.astype(F32)).astype(BF16)


def _dot_bf16x3(a, b):
    a_hi, a_lo = _split_bf16(a)
    b_hi, b_lo = _split_bf16(b)
    return _dot(jnp.concatenate([a_hi, a_lo], axis=1), jnp.concatenate([b_hi, b_hi], axis=0)) + _dot(a_hi, b_lo)


def _dot_nt(a, b, precision=None):
    return lax.dot_general(a, b, (((1,), (1,)), ((), ())), preferred_element_type=F32, precision=precision)


def _mm_kernel(a_ref, w_ref, o_ref):
    o_ref[...] = _dot(a_ref[...].astype(BF16), w_ref[...]).astype(o_ref.dtype)


def _matmul(a, w, out_dtype, tm=1024, tn=1024):
    n, k = a.shape
    nout = w.shape[1]
    tn = min(tn, nout)
    return pl.pallas_call(
        _mm_kernel,
        grid=(n // tm, nout // tn),
        in_specs=[pl.BlockSpec((tm, k), lambda i, j: (i, 0)), pl.BlockSpec((k, tn), lambda i, j: (0, j))],
        out_specs=pl.BlockSpec((tm, tn), lambda i, j: (i, j)),
        out_shape=jax.ShapeDtypeStruct((n, nout), out_dtype),
        compiler_params=_params(2),
        name="matmul",
    )(a, w)


def _mm_ln_kernel(a_ref, w_ref, x_ref, g_ref, b_ref, o_ref, acc_ref, *, alpha):
    @pl.when(pl.program_id(0) == 0)
    def _():
        acc_ref[...] = jnp.zeros_like(acc_ref)

    o_ref[...] = _layer_norm_rows(alpha * x_ref[...] + acc_ref[...], g_ref[...], b_ref[...])
    acc_ref[...] = _dot(a_ref[...].astype(BF16), w_ref[...])


def _matmul_ln(a, w, x, g, b, alpha, tm=512):
    n, k = a.shape
    d = w.shape[1]
    ni = n // tm
    cur = lambda s: jnp.minimum(s, ni - 1)
    lag = lambda s: jnp.maximum(s - 1, 0)
    return pl.pallas_call(
        functools.partial(_mm_ln_kernel, alpha=alpha),
        grid=(ni + 1,),
        in_specs=[
            pl.BlockSpec((tm, k), lambda s: (cur(s), 0)),
            pl.BlockSpec((k, d), lambda s: (0, 0)),
            pl.BlockSpec((tm, d), lambda s: (lag(s), 0)),
            pl.BlockSpec((1, d), lambda s: (0, 0)),
            pl.BlockSpec((1, d), lambda s: (0, 0)),
        ],
        out_specs=pl.BlockSpec((tm, d), lambda s: (lag(s), 0)),
        out_shape=jax.ShapeDtypeStruct((n, d), F32),
        scratch_shapes=[pltpu.VMEM((tm, d), F32)],
        compiler_params=_params(1),
        name="matmul_ln",
    )(a, w, x, g, b)


def _mlp_kernel(x_ref, w1_ref, w2_ref, g_ref, b_ref, o_ref, xb_ref, *, alpha, nf):
    f = pl.program_id(1)

    def partial_out(xb):
        h = jnp.maximum(_dot(xb, w1_ref[0]), 0.0)
        return _dot((h * h).astype(BF16), w2_ref[0])

    def first():
        xb = x_ref[...].astype(BF16)
        xb_ref[...] = xb
        return alpha * x_ref[...] + partial_out(xb)

    if nf == 1:
        o_ref[...] = _layer_norm_rows(first(), g_ref[...], b_ref[...])
        return

    @pl.when(f == 0)
    def _():
        o_ref[...] = first()

    @pl.when(jnp.logical_and(f > 0, f < nf - 1))
    def _():
        o_ref[...] += partial_out(xb_ref[...])

    @pl.when(f == nf - 1)
    def _():
        o_ref[...] = _layer_norm_rows(o_ref[...] + partial_out(xb_ref[...]), g_ref[...], b_ref[...])


def _mlp_ln(x, w1, w2, layer, g, b, alpha, tm=512, tf=2048):
    n, d = x.shape
    dff = w1.shape[2]
    tf = min(tf, dff)
    return pl.pallas_call(
        functools.partial(_mlp_kernel, alpha=alpha, nf=dff // tf),
        grid=(n // tm, dff // tf),
        in_specs=[
            pl.BlockSpec((tm, d), lambda i, f: (i, 0)),
            pl.BlockSpec((1, d, tf), lambda i, f: (layer, 0, f)),
            pl.BlockSpec((1, tf, d), lambda i, f: (layer, f, 0)),
            pl.BlockSpec((1, d), lambda i, f: (0, 0)),
            pl.BlockSpec((1, d), lambda i, f: (0, 0)),
        ],
        out_specs=pl.BlockSpec((tm, d), lambda i, f: (i, 0)),
        out_shape=jax.ShapeDtypeStruct((n, d), F32),
        scratch_shapes=[pltpu.VMEM((tm, d), BF16)],
        compiler_params=_params(2, MLP_VMEM_LIMIT_BYTES),
        name="mlp_ln",
    )(x, w1, w2, g, b)


def _rglru_kernel(gate_ref, u_ref, cw_ref, cb_ref, aw_ref, ab_ref, xw_ref, xb_ref, lam_ref, o_ref, hc_ref, ut_ref, *, ts):
    @pl.when(pl.program_id(2) == 0)
    def _():
        hc_ref[...] = jnp.zeros_like(hc_ref)
        ut_ref[...] = jnp.zeros_like(ut_ref)

    u = u_ref[...]
    ue = jnp.concatenate([ut_ref[...], u], axis=0)
    conv = cw_ref[CONV_WIDTH - 1 : CONV_WIDTH, :] * u + cb_ref[...]
    for d in range(1, CONV_WIDTH):
        conv = conv + cw_ref[CONV_WIDTH - 1 - d : CONV_WIDTH - d, :] * pltpu.roll(ue, d, axis=0)[SUBLANES:]
    ut_ref[...] = u[ts - SUBLANES :, :]

    cb16 = conv.astype(BF16)
    r = jax.nn.sigmoid(_dot(cb16, aw_ref[0]) + ab_ref[...])
    i = jax.nn.sigmoid(_dot(cb16, xw_ref[0]) + xb_ref[...])
    log_a = (-LRU_C) * r * _softplus(-lam_ref[...])
    a = jnp.exp(log_a)
    b = conv * i * jnp.sqrt(1.0 - a * a)

    w = a.shape[1]
    a = a.reshape(ts // SUBLANES, SUBLANES, w)
    b = b.reshape(ts // SUBLANES, SUBLANES, w)
    row = lax.broadcasted_iota(jnp.int32, a.shape, 1)
    d = 1
    while d < SUBLANES:
        keep = row >= d
        a_sh = jnp.where(keep, pltpu.roll(a, d, axis=1), 1.0)
        b_sh = jnp.where(keep, pltpu.roll(b, d, axis=1), 0.0)
        b = a * b_sh + b
        a = a * a_sh
        d *= 2
    h_prev = hc_ref[0:1, :]
    hs = []
    for g in range(ts // SUBLANES):
        hg = a[g] * h_prev + b[g]
        hs.append(hg)
        h_prev = hg[SUBLANES - 1 : SUBLANES, :]
    h = jnp.concatenate(hs, axis=0)
    hc_ref[...] = jnp.broadcast_to(h_prev, hc_ref.shape)
    o_ref[...] = (jax.nn.gelu(gate_ref[...]) * h).astype(o_ref.dtype)


def _rglru(proj, conv_w, conv_b, gate_a_w, gate_a_b, gate_x_w, gate_x_b, lam, bsz, seqlen, ts=2048):
    n, r2 = proj.shape
    r = r2 // 2
    nblk, w, _ = gate_a_w.shape
    ts = min(ts, seqlen)
    nt = seqlen // ts
    row_spec = pl.BlockSpec((1, w), lambda b, j, t: (0, j))
    gw_spec = pl.BlockSpec((1, w, w), lambda b, j, t: (j, 0, 0))
    return pl.pallas_call(
        functools.partial(_rglru_kernel, ts=ts),
        grid=(bsz, nblk, nt),
        in_specs=[
            pl.BlockSpec((ts, w), lambda b, j, t: (b * nt + t, j)),
            pl.BlockSpec((ts, w), lambda b, j, t: (b * nt + t, nblk + j)),
            pl.BlockSpec((CONV_WIDTH, w), lambda b, j, t: (0, j)),
            row_spec,
            gw_spec,
            row_spec,
            gw_spec,
            row_spec,
            row_spec,
        ],
        out_specs=pl.BlockSpec((ts, w), lambda b, j, t: (b * nt + t, j)),
        out_shape=jax.ShapeDtypeStruct((n, r), BF16),
        scratch_shapes=[pltpu.VMEM((SUBLANES, w), F32), pltpu.VMEM((SUBLANES, w), F32)],
        compiler_params=_params(3),
        name="rglru",
    )(proj, proj, conv_w, conv_b, gate_a_w, gate_a_b, gate_x_w, gate_x_b, lam)


def _qkv_kernel(x_ref, w_ref, c_ref, s1_ref, s2_ref, o_ref, acc_ref, xb_ref, *, tn, nj):
    @pl.when(pl.program_id(0) == 0)
    def _():
        acc_ref[...] = jnp.zeros_like(acc_ref)

    @pl.when(pl.program_id(0) % nj == 0)
    def _():
        xb_ref[...] = x_ref[...].astype(BF16)

    half = ROPE_DIM // 2
    rep = tn // ATTN_HEAD_DIM
    prev = acc_ref[...]
    c = jnp.concatenate([c_ref[0]] * rep, axis=1)
    s1 = jnp.concatenate([s1_ref[0]] * rep, axis=1)
    s2 = jnp.concatenate([s2_ref[0]] * rep, axis=1)
    roped = prev * c + pltpu.roll(prev, half, axis=1) * s1 + pltpu.roll(prev, tn - half, axis=1) * s2
    o_ref[...] = roped.astype(o_ref.dtype)
    acc_ref[...] = _dot(xb_ref[...], w_ref[...])


def _rope_tables(seqlen, scale):
    half = ROPE_DIM // 2
    pos = jnp.arange(seqlen, dtype=F32)
    inv_freq = ROPE_THETA ** (-jnp.arange(0, ROPE_DIM, 2, dtype=F32) / ROPE_DIM)
    ang = pos[:, None] * inv_freq[None, :]
    cos, sin = jnp.cos(ang), jnp.sin(ang)
    pad = ATTN_HEAD_DIM - ROPE_DIM
    c = jnp.concatenate([cos, cos, jnp.ones((seqlen, pad), F32)], axis=1)
    s1 = jnp.concatenate([jnp.zeros((seqlen, half), F32), sin, jnp.zeros((seqlen, pad), F32)], axis=1)
    s2 = jnp.concatenate([-sin, jnp.zeros((seqlen, half + pad), F32)], axis=1)
    ident = jnp.ones_like(c)
    zero = jnp.zeros_like(c)
    return (jnp.stack([c * scale, c, ident]), jnp.stack([s1 * scale, s1, zero]), jnp.stack([s2 * scale, s2, zero]))


def _qkv_rope(x, w_qkv, seqlen, tm=512, tn=2048):
    n, d = x.shape
    nout = w_qkv.shape[1]
    tm = min(tm, seqlen)
    tn = min(tn, d)
    c, s1, s2 = _rope_tables(seqlen, ATTN_HEAD_DIM**-0.5)
    nt = seqlen // tm
    ni, nj = n // tm, nout // tn
    last = ni * nj - 1
    cur = lambda s: jnp.minimum(s, last)
    lag = lambda s: jnp.maximum(s - 1, 0)
    tab_spec = pl.BlockSpec((1, tm, ATTN_HEAD_DIM), lambda s: ((lag(s) % nj) * tn // d, (lag(s) // nj) % nt, 0))
    return pl.pallas_call(
        functools.partial(_qkv_kernel, tn=tn, nj=nj),
        grid=(ni * nj + 1,),
        in_specs=[
            pl.BlockSpec((tm, d), lambda s: (cur(s) // nj, 0)),
            pl.BlockSpec((d, tn), lambda s: (0, cur(s) % nj)),
            tab_spec,
            tab_spec,
            tab_spec,
        ],
        out_specs=pl.BlockSpec((tm, tn), lambda s: (lag(s) // nj, lag(s) % nj)),
        out_shape=jax.ShapeDtypeStruct((n, nout), BF16),
        scratch_shapes=[pltpu.VMEM((tm, tn), F32), pltpu.VMEM((tm, d), BF16)],
        compiler_params=_params(1),
        name="qkv_rope",
    )(x, w_qkv, c, s1, s2)


def _moba_kernel(q_ref, k_ref, v_ref, o_ref, kmean_ref, vt_ref, sel_ref, s_ref, *, nblk, nh, nq):
    t = pl.program_id(2)
    blk, hd = MOBA_BLOCK, ATTN_HEAD_DIM

    heads = range(nh)
    tiles = range(nq)
    combos = [(a, h) for a in tiles for h in heads]
    lanes = [slice(h * hd, (h + 1) * hd) for h in heads]
    rows = [slice(a * blk, (a + 1) * blk) for a in tiles]
    own = [nq * t + a for a in tiles]

    @pl.when(t == 0)
    def _():
        for h in heads:
            kf = k_ref[:, lanes[h]].astype(F32).reshape(nblk, blk, hd)
            kmean_ref[h] = jnp.sum(kf, axis=1) * (1.0 / blk)
            vt_ref[h] = v_ref[:, lanes[h]].astype(F32).T.astype(BF16)

    q = {(a, h): q_ref[rows[a], lanes[h]] for a, h in combos}

    def scores(jp, a, h):
        off = pl.multiple_of(jp * (2 * blk), 2 * blk)
        return _dot_nt(k_ref[pl.ds(off, 2 * blk), lanes[h]], q[a, h])

    for a, h in combos:
        s_ref[a, 0, h] = scores(0, a, h)

    for a, h in combos:
        gate = _dot_nt(kmean_ref[h], q[a, h].astype(F32), precision=HIGHEST)
        blk_id = lax.broadcasted_iota(jnp.int32, gate.shape, 0).astype(F32)
        past = blk_id < own[a].astype(F32)
        g = jnp.where(past, gate, NEG_INF)
        sel = jnp.zeros_like(gate)
        for _ in range(MOBA_TOPK):
            top = jnp.max(g, axis=0, keepdims=True)
            first = jnp.min(jnp.where(g == top, blk_id, float(nblk)), axis=0, keepdims=True)
            hit = blk_id == first
            sel = jnp.where(hit, 1.0, sel)
            g = jnp.where(hit, MOBA_TAKEN, g)
        sel_ref[a, h] = jnp.where(past, sel, 0.0)

    key_pos = lax.broadcasted_iota(jnp.int32, (blk, blk), 0)
    qry_pos = lax.broadcasted_iota(jnp.int32, (blk, blk), 1)
    carry = {}
    for a, h in combos:
        off_own = pl.multiple_of(own[a] * blk, blk)
        s = _dot_nt(k_ref[pl.ds(off_own, blk), lanes[h]], q[a, h])
        s = jnp.where(key_pos <= qry_pos, s, NEG_INF)
        m = jnp.max(s, axis=0, keepdims=True)
        p = jnp.exp(s - m)
        l = jnp.sum(p, axis=0, keepdims=True)
        acc = _dot(vt_ref[h, :, pl.ds(off_own, blk)], p.astype(BF16))
        carry[a, h] = (m, l, acc)

    def make_body(a):
        def body(jp, carry):
            off = pl.multiple_of(jp * (2 * blk), 2 * blk)
            s_cur = [s_ref[a, jp % 2, h] for h in heads]
            jn = jnp.minimum(jp + 1, nblk // 2 - 1)
            for h in heads:
                s_ref[a, (jp + 1) % 2, h] = scores(jn, a, h)
            out = []
            for h in heads:
                m, l, acc = carry[h]
                s0 = jnp.where(sel_ref[a, h, pl.ds(2 * jp, 1), :] > 0.5, s_cur[h][:blk], NEG_INF)
                s1 = jnp.where(sel_ref[a, h, pl.ds(2 * jp + 1, 1), :] > 0.5, s_cur[h][blk:], NEG_INF)
                m_blk = jnp.maximum(jnp.max(s0, axis=0, keepdims=True), jnp.max(s1, axis=0, keepdims=True))
                m_new = jnp.maximum(m, m_blk)
                alpha = jnp.exp(m - m_new)
                p0 = jnp.exp(s0 - m_new)
                p1 = jnp.exp(s1 - m_new)
                l = alpha * l + jnp.sum(p0, axis=0, keepdims=True) + jnp.sum(p1, axis=0, keepdims=True)
                p = jnp.concatenate([p0.astype(BF16), p1.astype(BF16)], axis=0)
                acc = acc * alpha + _dot(vt_ref[h, :, pl.ds(off, 2 * blk)], p)
                out.append((m_new, l, acc))
            return tuple(out)

        return body

    done = {}
    for a in tiles:
        res = lax.fori_loop(0, (own[a] + 1) // 2, make_body(a), tuple(carry[a, h] for h in heads))
        for h in heads:
            done[a, h] = res[h]
    for a, h in combos:
        m, l, acc = done[a, h]
        o_ref[rows[a], lanes[h]] = (acc / l).T.astype(o_ref.dtype)


def _moba_attention(qkv, bsz, seqlen, nh=2, nq=8):
    n, d3 = qkv.shape
    d = d3 // 3
    ngrp = d // (nh * ATTN_HEAD_DIM)
    nblk = seqlen // MOBA_BLOCK
    assert nblk % 2 == 0 and nblk % nq == 0
    blk, hd = MOBA_BLOCK, ATTN_HEAD_DIM
    wl = nh * hd
    nt = nblk // nq
    return pl.pallas_call(
        functools.partial(_moba_kernel, nblk=nblk, nh=nh, nq=nq),
        grid=(bsz, ngrp, nt),
        in_specs=[
            pl.BlockSpec((nq * blk, wl), lambda b, h, t: (b * nt + t, h)),
            pl.BlockSpec((seqlen, wl), lambda b, h, t: (b, ngrp + h)),
            pl.BlockSpec((seqlen, wl), lambda b, h, t: (b, 2 * ngrp + h)),
        ],
        out_specs=pl.BlockSpec((nq * blk, wl), lambda b, h, t: (b * nt + t, h)),
        out_shape=jax.ShapeDtypeStruct((n, d), BF16),
        scratch_shapes=[pltpu.VMEM((nh, nblk, hd), F32), pltpu.VMEM((nh, hd, seqlen), BF16),
                        pltpu.VMEM((nq, nh, nblk, blk), F32), pltpu.VMEM((nq, 2, nh, 2 * blk, blk), F32)],
        compiler_params=_params(3),
        name="moba_attention",
    )(qkv, qkv, qkv)


def _token_shift_delta(x, prev_ref, is_start):
    prev_row = jnp.where(is_start, 0.0, prev_ref[SUBLANES - 1 : SUBLANES, :])
    row = lax.broadcasted_iota(jnp.int32, x.shape, 0)
    return jnp.where(row == 0, prev_row, pltpu.roll(x, 1, axis=0)) - x


def _rwkv_proj_kernel(x_ref, prev_ref, mu_ref, w_ref, o_ref, *, tm, seqlen):
    x = x_ref[...]
    xx = _token_shift_delta(x, prev_ref, (pl.program_id(1) * tm) % seqlen == 0)
    o_ref[0] = _dot((x + xx * mu_ref[0]).astype(BF16), w_ref[0])


def _rwkv_proj(x, mu, w_rkv, seqlen, tm=512):
    n, d = x.shape
    tm = min(tm, seqlen)
    step = tm // SUBLANES
    return pl.pallas_call(
        functools.partial(_rwkv_proj_kernel, tm=tm, seqlen=seqlen),
        grid=(3, n // tm),
        in_specs=[
            pl.BlockSpec((tm, d), lambda g, i: (i, 0)),
            pl.BlockSpec((SUBLANES, d), lambda g, i: (jnp.maximum(i * step - 1, 0), 0)),
            pl.BlockSpec((1, 1, d), lambda g, i: (g, 0, 0)),
            pl.BlockSpec((1, d, d), lambda g, i: (g, 0, 0)),
        ],
        out_specs=pl.BlockSpec((1, tm, d), lambda g, i: (g, i, 0)),
        out_shape=jax.ShapeDtypeStruct((3, n, d), F32),
        compiler_params=_params(2),
        name="rwkv_proj",
    )(x, x, mu.reshape(mu.shape[0], 1, d), w_rkv)


def _rwkv_lora_kernel(x_ref, prev_ref, mu_ref, ww1_ref, ww2_ref, aw1_ref, aw2_ref, gw1_ref, gw2_ref, wo_ref, ao_ref, go_ref, *, tm, seqlen):
    x = x_ref[...]
    xx = _token_shift_delta(x, prev_ref, (pl.program_id(0) * tm) % seqlen == 0)
    xw = (x + xx * mu_ref[3:4, :]).astype(BF16)
    xa = (x + xx * mu_ref[4:5, :]).astype(BF16)
    xg = (x + xx * mu_ref[5:6, :]).astype(BF16)
    wo_ref[...] = _dot(jnp.tanh(_dot(xw, ww1_ref[...])).astype(BF16), ww2_ref[...])
    ao_ref[...] = _dot(_dot(xa, aw1_ref[...]).astype(BF16), aw2_ref[...])
    go_ref[...] = _dot(jax.nn.sigmoid(_dot(xg, gw1_ref[...])).astype(BF16), gw2_ref[...])


def _pad_lora(w1, w2):
    rank = w1.shape[1]
    pad = (-rank) % LANES
    return jnp.pad(w1, ((0, 0), (0, pad))), jnp.pad(w2, ((0, pad), (0, 0)))


def _rwkv_lora(x, mu, w_w1, w_w2, a_w1, a_w2, g_w1, g_w2, seqlen, tm=512):
    n, d = x.shape
    tm = min(tm, seqlen)
    step = tm // SUBLANES
    ws = [*_pad_lora(w_w1, w_w2), *_pad_lora(a_w1, a_w2), *_pad_lora(g_w1, g_w2)]
    full = lambda a: pl.BlockSpec(a.shape, lambda i: (0, 0))
    out_spec = pl.BlockSpec((tm, d), lambda i: (i, 0))
    return pl.pallas_call(
        functools.partial(_rwkv_lora_kernel, tm=tm, seqlen=seqlen),
        grid=(n // tm,),
        in_specs=[
            pl.BlockSpec((tm, d), lambda i: (i, 0)),
            pl.BlockSpec((SUBLANES, d), lambda i: (jnp.maximum(i * step - 1, 0), 0)),
            full(mu),
            *[full(w) for w in ws],
        ],
        out_specs=[out_spec, out_spec, out_spec],
        out_shape=[jax.ShapeDtypeStruct((n, d), F32)] * 3,
        compiler_params=_params(1),
        name="rwkv_lora",
    )(x, x, mu, *ws)


def _rwkv_core_kernel(r_ref, k_ref, v_ref, wpre_ref, apre_ref, g_ref, w0_ref, a0_ref, kk_ref, ka_ref, rk_ref, lg_ref, lb_ref,
                      o_ref, st_ref, *, tb, npp):
    c = RWKV_CHUNK
    hd = RWKV_HEAD_DIM

    @pl.when(pl.program_id(2) == 0)
    def _():
        st_ref[...] = jnp.zeros_like(st_ref)

    lane = lax.broadcasted_iota(jnp.int32, (1, RWKV_PAIR), 1)
    m0 = jnp.where(lane < hd, 1.0, 0.0)
    m1 = 1.0 - m0
    ri = lax.broadcasted_iota(jnp.int32, (RWKV_PAIR, RWKV_PAIR), 0)
    ci = lax.broadcasted_iota(jnp.int32, (RWKV_PAIR, RWKV_PAIR), 1)
    same = (ri < hd) == (ci < hd)
    ones_bd = jnp.where(same, 1.0, 0.0)
    strict = jnp.where(same & (ci < ri), 1.0, 0.0)
    incl = jnp.where(same & (ci <= ri), 1.0, 0.0)
    eye = jnp.where(ri == ci, 1.0, 0.0)

    def head_sum(z):
        hi, lo = _split_bf16(z)
        return _dot(jnp.concatenate([hi, lo], axis=1), ones2_b16)

    def stack(z):
        return jnp.concatenate([z * m0, z * m1], axis=0)

    def b16(z):
        return z.astype(BF16)

    ones2_b16 = b16(jnp.concatenate([ones_bd, ones_bd], axis=0))
    zeros_sq = jnp.zeros((RWKV_PAIR, RWKV_PAIR), F32)

    n2 = RWKV_PAIR
    nch = tb // c
    rowc = lax.broadcasted_iota(jnp.int32, (tb, n2), 0) % c

    prep = []
    for p in range(npp):
        ln = slice(p * n2, (p + 1) * n2)
        r = r_ref[0, :, ln]
        k = k_ref[0, :, ln]
        v = v_ref[0, :, ln]
        z = w0_ref[:, ln] + wpre_ref[:, ln]
        w_log = jnp.minimum(z, 0.0) - jnp.log(1.0 + jnp.exp(-jnp.abs(z))) - 0.5
        lw = -jnp.exp(w_log)
        a = jax.nn.sigmoid(a0_ref[:, ln] + apre_ref[:, ln])
        kk = k * kk_ref[:, ln]
        kk = kk / jnp.maximum(jnp.sqrt(head_sum(kk * kk)), 1e-12)
        k2 = k * (1.0 + (a - 1.0) * ka_ref[:, ln])
        bv = kk * a
        cl = lw
        d = 1
        while d < c:
            cl = cl + jnp.where(rowc >= d, pltpu.roll(cl, d, axis=0), 0.0)
            d *= 2
        prep.append((r, k2, v, kk, bv, lw, cl))

    items = [(p, ch) for p in range(npp) for ch in range(nch)]
    rt, at16, v16, lhs16, diag, a_ab, a_ak16 = {}, {}, {}, {}, {}, {}, {}
    for it in items:
        p, ch = it
        r, k2, v, kk, bv, lw, cl = prep[p]
        sl = slice(ch * c, (ch + 1) * c)
        clc = cl[sl]
        cle = clc[c - 1 : c, :]
        e_neg = jnp.exp(-clc)
        e_end = jnp.exp(cle - clc)
        rt_s = stack(r[sl] * jnp.exp(clc))
        at_s = stack(-kk[sl] * jnp.exp(clc - lw[sl]))
        gmat = _dot_nt(b16(jnp.concatenate([at_s, rt_s], axis=0)),
                       b16(jnp.concatenate([stack(bv[sl] * e_neg), stack(k2[sl] * e_neg)], axis=0)))
        a_ab[it] = gmat[:n2, :n2] * strict
        a_ak16[it] = b16(gmat[:n2, n2:] * strict)
        ends_t = jnp.concatenate([stack(bv[sl] * e_end).T, stack(k2[sl] * e_end).T], axis=1)
        lhs16[it] = b16(jnp.concatenate([gmat[n2:] * jnp.concatenate([incl, incl], axis=1), ends_t], axis=0))
        rt[it] = rt_s
        at16[it] = b16(at_s)
        v16[it] = b16(stack(v[sl]))
        diag[it] = eye * jnp.exp(cle)

    av16 = {it: b16(_dot(a_ak16[it], v16[it])) for it in items}
    tinv = {it: eye + a_ab[it] for it in items}
    l16 = {it: b16(a_ab[it]) for it in items}
    pw = {it: _dot(l16[it], l16[it]) for it in items}
    span = 2
    while span < c:
        span *= 2
        for it in items:
            p16 = b16(pw[it])
            if span < c:
                res = _dot(p16, jnp.concatenate([b16(tinv[it]), p16], axis=1))
                tinv[it] = tinv[it] + res[:, :n2]
                pw[it] = res[:, n2:]
            else:
                tinv[it] = tinv[it] + _dot(p16, b16(tinv[it]))

    big = {}
    for it in items:
        wu16 = b16(_dot(b16(tinv[it]), jnp.concatenate([at16[it], av16[it]], axis=1)))
        rhs16 = jnp.concatenate([wu16, jnp.concatenate([b16(zeros_sq), v16[it]], axis=1)], axis=0)
        big[it] = _dot(lhs16[it], rhs16)

    st = {p: st_ref[p] for p in range(npp)}
    st_in = {}
    for ch in range(nch):
        for p in range(npp):
            it = (p, ch)
            st_in[it] = st[p]
            st[p] = _dot_bf16x3(diag[it] + big[it][n2:, :n2], st[p]) + big[it][n2:, n2:]
    for p in range(npp):
        st_ref[p] = st[p]

    for p in range(npp):
        ln = slice(p * n2, (p + 1) * n2)
        r, k2, v, kk, bv, lw, cl = prep[p]
        ys = []
        for ch in range(nch):
            it = (p, ch)
            y_s = _dot(b16(rt[it] + big[it][:n2, :n2]), b16(st_in[it])) + big[it][:n2, n2:]
            ys.append(y_s[:c] + y_s[c:])
        y = jnp.concatenate(ys, axis=0)
        mean = head_sum(y) * (1.0 / hd)
        yc = y - mean
        var = head_sum(yc * yc) * (1.0 / hd)
        yn = yc * lax.rsqrt(var + RWKV_LNX_EPS) * lg_ref[:, ln] + lb_ref[:, ln]
        bonus = head_sum(r * k2 * rk_ref[:, ln]) * v
        o_ref[:, ln] = ((yn + bonus) * g_ref[:, ln]).astype(o_ref.dtype)


def _rwkv_core(rkv, w_pre, a_pre, g, w0, a0, k_k, k_a, r_k, lnx_g, lnx_b, bsz, seqlen, tb=512, npp=4):
    _, n, d = rkv.shape
    tb = min(tb, seqlen)
    nt = seqlen // tb
    wl = npp * RWKV_PAIR
    rkv_spec = lambda j: pl.BlockSpec((1, tb, wl), lambda b, p, t: (j, b * nt + t, p))
    act_spec = pl.BlockSpec((tb, wl), lambda b, p, t: (b * nt + t, p))
    row_spec = pl.BlockSpec((1, wl), lambda b, p, t: (0, p))
    return pl.pallas_call(
        functools.partial(_rwkv_core_kernel, tb=tb, npp=npp),
        grid=(bsz, d // wl, nt),
        in_specs=[rkv_spec(0), rkv_spec(1), rkv_spec(2), act_spec, act_spec, act_spec] + [row_spec] * 7,
        out_specs=act_spec,
        out_shape=jax.ShapeDtypeStruct((n, d), BF16),
        scratch_shapes=[pltpu.VMEM((npp, RWKV_PAIR, RWKV_PAIR), F32)],
        compiler_params=_params(3),
        name="rwkv_core",
    )(rkv, rkv, rkv, w_pre, a_pre, g, w0, a0, k_k, k_a, r_k, lnx_g, lnx_b)


def _pool_kernel(x_ref, prev_ref, w_ref, sc_ref, g_ref, b_ref, o_ref, *, tm, seqlen, alpha):
    start = (pl.program_id(0) * tm) % seqlen
    x = x_ref[...]
    prev = jnp.where(start == 0, 0.0, prev_ref[...])
    xe = jnp.concatenate([prev, x], axis=0)
    pos = start + lax.broadcasted_iota(jnp.int32, (tm, 1), 0)
    gw = w_ref.shape[1]
    ys = []
    for gi, win in enumerate(POOL_WINDOWS):
        sl = slice(gi * gw, (gi + 1) * gw)
        s = xe[:, sl]
        d = 1
        while d < win:
            s = s + pltpu.roll(s, d, axis=0)
            d *= 2
        cnt = jnp.minimum(pos + 1, win).astype(F32)
        pooled = s[POOL_HALO:] / cnt - x[:, sl]
        ys.append(_dot(pooled.astype(BF16), w_ref[gi]))
    y = jnp.concatenate(ys, axis=1) * sc_ref[...]
    o_ref[...] = _layer_norm_rows(alpha * x + y, g_ref[...], b_ref[...])


def _pool_ln(x, w_pool, scale, g, b, seqlen, alpha, tm=512):
    n, d = x.shape
    tm = min(tm, seqlen)
    step = tm // POOL_HALO
    row_spec = pl.BlockSpec((1, d), lambda i: (0, 0))
    return pl.pallas_call(
        functools.partial(_pool_kernel, tm=tm, seqlen=seqlen, alpha=alpha),
        grid=(n // tm,),
        in_specs=[
            pl.BlockSpec((tm, d), lambda i: (i, 0)),
            pl.BlockSpec((POOL_HALO, d), lambda i: (jnp.maximum(i * step - 1, 0), 0)),
            pl.BlockSpec(w_pool.shape, lambda i: (0, 0, 0)),
            row_spec,
            row_spec,
            row_spec,
        ],
        out_specs=pl.BlockSpec((tm, d), lambda i: (i, 0)),
        out_shape=jax.ShapeDtypeStruct((n, d), F32),
        compiler_params=_params(1),
        name="pool_ln",
    )(x, x, w_pool, scale, g, b)


def kernel(x, ln_g, ln_b, mlp_w1, mlp_w2, rg_w_in, rg_conv_w, rg_conv_b, rg_gate_a_w, rg_gate_a_b, rg_gate_x_w, rg_gate_x_b, rg_lambda, rg_w_out, moba_w_qkv, moba_w_out, rwkv_mu, rwkv_w_rkv, rwkv_w0, rwkv_w_w1, rwkv_w_w2, rwkv_a0, rwkv_a_w1, rwkv_a_w2, rwkv_g_w1, rwkv_g_w2, rwkv_k_k, rwkv_k_a, rwkv_r_k, rwkv_lnx_g, rwkv_lnx_b, rwkv_w_out, pool_w, pool_scale):
    bsz, seqlen, d = x.shape
    depth = ln_g.shape[0]
    n_mixers = 4
    alpha = (2.0 * depth) ** 0.25
    bf = lambda w: w.astype(BF16)
    row = lambda p: p.reshape(1, -1)
    mlp_w1_b16, mlp_w2_b16 = bf(mlp_w1), bf(mlp_w2)

    h = x.reshape(bsz * seqlen, d)
    counts = [0] * n_mixers
    for layer in range(depth):
        m = layer % n_mixers
        j = counts[m]
        counts[m] += 1
        g0, b0 = row(ln_g[layer, 0]), row(ln_b[layer, 0])
        if m == 0:
            proj = _matmul(h, bf(rg_w_in[j]), F32)
            mixed = _rglru(proj, rg_conv_w[j], row(rg_conv_b[j]), bf(rg_gate_a_w[j]), row(rg_gate_a_b[j]),
                           bf(rg_gate_x_w[j]), row(rg_gate_x_b[j]), row(rg_lambda[j]), bsz, seqlen)
            h = _matmul_ln(mixed, bf(rg_w_out[j]), h, g0, b0, alpha)
        elif m == 1:
            qkv = _qkv_rope(h, bf(moba_w_qkv[j]), seqlen)
            att = _moba_attention(qkv, bsz, seqlen)
            h = _matmul_ln(att, bf(moba_w_out[j]), h, g0, b0, alpha)
        elif m == 2:
            rkv = _rwkv_proj(h, rwkv_mu[j], bf(rwkv_w_rkv[j]), seqlen)
            w_pre, a_pre, gate = _rwkv_lora(h, rwkv_mu[j], bf(rwkv_w_w1[j]), bf(rwkv_w_w2[j]), bf(rwkv_a_w1[j]),
                                            bf(rwkv_a_w2[j]), bf(rwkv_g_w1[j]), bf(rwkv_g_w2[j]), seqlen)
            mixed = _rwkv_core(rkv, w_pre, a_pre, gate, row(rwkv_w0[j]), row(rwkv_a0[j]), row(rwkv_k_k[j]), row(rwkv_k_a[j]),
                               row(rwkv_r_k[j]), row(rwkv_lnx_g[j]), row(rwkv_lnx_b[j]), bsz, seqlen)
            h = _matmul_ln(mixed, bf(rwkv_w_out[j]), h, g0, b0, alpha)
        else:
            h = _pool_ln(h, bf(pool_w[j]), row(pool_scale[j]), g0, b0, seqlen, alpha)
        h = _mlp_ln(h, mlp_w1_b16, mlp_w2_b16, layer, row(ln_g[layer, 1]), row(ln_b[layer, 1]), alpha)
    return h.reshape(bsz, seqlen, d)
```

```python
import functools

import jax
import jax.numpy as jnp
from jax import lax
from jax.experimental import pallas as pl
from jax.experimental.pallas import tpu as pltpu

F32 = jnp.float32
BF16 = jnp.bfloat16
HIGHEST = lax.Precision.HIGHEST

LN_EPS = 1e-5
NEG_INF = -1e30
MOBA_TAKEN = -3e38
VMEM_LIMIT_BYTES = 48 * 1024 * 1024
MLP_VMEM_LIMIT_BYTES = 58 * 1024 * 1024

LRU_C = 8.0
CONV_WIDTH = 4
ATTN_HEAD_DIM = 128
MOBA_BLOCK = 256
MOBA_TOPK = 3
ROPE_THETA = 500000.0
ROPE_DIM = ATTN_HEAD_DIM // 4
RWKV_HEAD_DIM = 64
RWKV_PAIR = 2 * RWKV_HEAD_DIM
RWKV_CHUNK = 64
RWKV_LNX_EPS = 64e-5
POOL_WINDOWS = (2, 4, 8, 16)
POOL_HALO = 16
LANES = 128
SUBLANES = 8


def _params(n_axes, vmem_limit_bytes=VMEM_LIMIT_BYTES):
    return pltpu.CompilerParams(dimension_semantics=("arbitrary",) * n_axes, vmem_limit_bytes=vmem_limit_bytes)


def _layer_norm_rows(z, g, b):
    mean = jnp.mean(z, axis=-1, keepdims=True)
    zc = z - mean
    var = jnp.mean(zc * zc, axis=-1, keepdims=True)
    return zc * lax.rsqrt(var + LN_EPS) * g + b


def _softplus(z):
    return jnp.maximum(z, 0.0) + jnp.log1p(jnp.exp(-jnp.abs(z)))


def _dot(a, b, precision=None):
    return jnp.dot(a, b, preferred_element_type=F32, precision=precision)


def _split_bf16(z):
    hi = z.astype(BF16)
    return hi, (z - hi.astype(F32)).astype(BF16)


def _dot_bf16x3(a, b):
    a_hi, a_lo = _split_bf16(a)
    b_hi, b_lo = _split_bf16(b)
    return _dot(jnp.concatenate([a_hi, a_lo], axis=1), jnp.concatenate([b_hi, b_hi], axis=0)) + _dot(a_hi, b_lo)


def _dot_nt(a, b, precision=None):
    return lax.dot_general(a, b, (((1,), (1,)), ((), ())), preferred_element_type=F32, precision=precision)


def _mm_kernel(a_ref, w_ref, o_ref):
    o_ref[...] = _dot(a_ref[...].astype(BF16), w_ref[...]).astype(o_ref.dtype)


def _matmul(a, w, out_dtype, tm=1024, tn=1024):
    n, k = a.shape
    nout = w.shape[1]
    tn = min(tn, nout)
    return pl.pallas_call(
        _mm_kernel,
        grid=(n // tm, nout // tn),
        in_specs=[pl.BlockSpec((tm, k), lambda i, j: (i, 0)), pl.BlockSpec((k, tn), lambda i, j: (0, j))],
        out_specs=pl.BlockSpec((tm, tn), lambda i, j: (i, j)),
        out_shape=jax.ShapeDtypeStruct((n, nout), out_dtype),
        compiler_params=_params(2),
        name="matmul",
    )(a, w)


def _mm_ln_kernel(a_ref, w_ref, x_ref, g_ref, b_ref, o_ref, acc_ref, *, alpha):
    @pl.when(pl.program_id(0) == 0)
    def _():
        acc_ref[...] = jnp.zeros_like(acc_ref)

    o_ref[...] = _layer_norm_rows(alpha * x_ref[...] + acc_ref[...], g_ref[...], b_ref[...])
    acc_ref[...] = _dot(a_ref[...].astype(BF16), w_ref[...])


def _matmul_ln(a, w, x, g, b, alpha, tm=512):
    n, k = a.shape
    d = w.shape[1]
    ni = n // tm
    cur = lambda s: jnp.minimum(s, ni - 1)
    lag = lambda s: jnp.maximum(s - 1, 0)
    return pl.pallas_call(
        functools.partial(_mm_ln_kernel, alpha=alpha),
        grid=(ni + 1,),
        in_specs=[
            pl.BlockSpec((tm, k), lambda s: (cur(s), 0)),
            pl.BlockSpec((k, d), lambda s: (0, 0)),
            pl.BlockSpec((tm, d), lambda s: (lag(s), 0)),
            pl.BlockSpec((1, d), lambda s: (0, 0)),
            pl.BlockSpec((1, d), lambda s: (0, 0)),
        ],
        out_specs=pl.BlockSpec((tm, d), lambda s: (lag(s), 0)),
        out_shape=jax.ShapeDtypeStruct((n, d), F32),
        scratch_shapes=[pltpu.VMEM((tm, d), F32)],
        compiler_params=_params(1),
        name="matmul_ln",
    )(a, w, x, g, b)


def _mlp_kernel(x_ref, w1_ref, w2_ref, g_ref, b_ref, o_ref, xb_ref, *, alpha, nf):
    f = pl.program_id(1)

    def partial_out(xb):
        h = jnp.maximum(_dot(xb, w1_ref[0]), 0.0)
        return _dot((h * h).astype(BF16), w2_ref[0])

    def first():
        xb = x_ref[...].astype(BF16)
        xb_ref[...] = xb
        return alpha * x_ref[...] + partial_out(xb)

    if nf == 1:
        o_ref[...] = _layer_norm_rows(first(), g_ref[...], b_ref[...])
        return

    @pl.when(f == 0)
    def _():
        o_ref[...] = first()

    @pl.when(jnp.logical_and(f > 0, f < nf - 1))
    def _():
        o_ref[...] += partial_out(xb_ref[...])

    @pl.when(f == nf - 1)
    def _():
        o_ref[...] = _layer_norm_rows(o_ref[...] + partial_out(xb_ref[...]), g_ref[...], b_ref[...])


def _mlp_ln(x, w1, w2, layer, g, b, alpha, tm=512, tf=2048):
    n, d = x.shape
    dff = w1.shape[2]
    tf = min(tf, dff)
    return pl.pallas_call(
        functools.partial(_mlp_kernel, alpha=alpha, nf=dff // tf),
        grid=(n // tm, dff // tf),
        in_specs=[
            pl.BlockSpec((tm, d), lambda i, f: (i, 0)),
            pl.BlockSpec((1, d, tf), lambda i, f: (layer, 0, f)),
            pl.BlockSpec((1, tf, d), lambda i, f: (layer, f, 0)),
            pl.BlockSpec((1, d), lambda i, f: (0, 0)),
            pl.BlockSpec((1, d), lambda i, f: (0, 0)),
        ],
        out_specs=pl.BlockSpec((tm, d), lambda i, f: (i, 0)),
        out_shape=jax.ShapeDtypeStruct((n, d), F32),
        scratch_shapes=[pltpu.VMEM((tm, d), BF16)],
        compiler_params=_params(2, MLP_VMEM_LIMIT_BYTES),
        name="mlp_ln",
    )(x, w1, w2, g, b)


def _rglru_kernel(gate_ref, u_ref, cw_ref, cb_ref, aw_ref, ab_ref, xw_ref, xb_ref, lam_ref, o_ref, hc_ref, ut_ref, *, ts):
    @pl.when(pl.program_id(2) == 0)
    def _():
        hc_ref[...] = jnp.zeros_like(hc_ref)
        ut_ref[...] = jnp.zeros_like(ut_ref)

    u = u_ref[...]
    ue = jnp.concatenate([ut_ref[...], u], axis=0)
    conv = cw_ref[CONV_WIDTH - 1 : CONV_WIDTH, :] * u + cb_ref[...]
    for d in range(1, CONV_WIDTH):
        conv = conv + cw_ref[CONV_WIDTH - 1 - d : CONV_WIDTH - d, :] * pltpu.roll(ue, d, axis=0)[SUBLANES:]
    ut_ref[...] = u[ts - SUBLANES :, :]

    cb16 = conv.astype(BF16)
    r = jax.nn.sigmoid(_dot(cb16, aw_ref[0]) + ab_ref[...])
    i = jax.nn.sigmoid(_dot(cb16, xw_ref[0]) + xb_ref[...])
    log_a = (-LRU_C) * r * _softplus(-lam_ref[...])
    a = jnp.exp(log_a)
    b = conv * i * jnp.sqrt(1.0 - a * a)

    w = a.shape[1]
    a = a.reshape(ts // SUBLANES, SUBLANES, w)
    b = b.reshape(ts // SUBLANES, SUBLANES, w)
    row = lax.broadcasted_iota(jnp.int32, a.shape, 1)
    d = 1
    while d < SUBLANES:
        keep = row >= d
        a_sh = jnp.where(keep, pltpu.roll(a, d, axis=1), 1.0)
        b_sh = jnp.where(keep, pltpu.roll(b, d, axis=1), 0.0)
        b = a * b_sh + b
        a = a * a_sh
        d *= 2
    h_prev = hc_ref[0:1, :]
    hs = []
    for g in range(ts // SUBLANES):
        hg = a[g] * h_prev + b[g]
        hs.append(hg)
        h_prev = hg[SUBLANES - 1 : SUBLANES, :]
    h = jnp.concatenate(hs, axis=0)
    hc_ref[...] = jnp.broadcast_to(h_prev, hc_ref.shape)
    o_ref[...] = (jax.nn.gelu(gate_ref[...]) * h).astype(o_ref.dtype)


def _rglru(proj, conv_w, conv_b, gate_a_w, gate_a_b, gate_x_w, gate_x_b, lam, bsz, seqlen, ts=2048):
    n, r2 = proj.shape
    r = r2 // 2
    nblk, w, _ = gate_a_w.shape
    ts = min(ts, seqlen)
    nt = seqlen // ts
    row_spec = pl.BlockSpec((1, w), lambda b, j, t: (0, j))
    gw_spec = pl.BlockSpec((1, w, w), lambda b, j, t: (j, 0, 0))
    return pl.pallas_call(
        functools.partial(_rglru_kernel, ts=ts),
        grid=(bsz, nblk, nt),
        in_specs=[
            pl.BlockSpec((ts, w), lambda b, j, t: (b * nt + t, j)),
            pl.BlockSpec((ts, w), lambda b, j, t: (b * nt + t, nblk + j)),
            pl.BlockSpec((CONV_WIDTH, w), lambda b, j, t: (0, j)),
            row_spec,
            gw_spec,
            row_spec,
            gw_spec,
            row_spec,
            row_spec,
        ],
        out_specs=pl.BlockSpec((ts, w), lambda b, j, t: (b * nt + t, j)),
        out_shape=jax.ShapeDtypeStruct((n, r), BF16),
        scratch_shapes=[pltpu.VMEM((SUBLANES, w), F32), pltpu.VMEM((SUBLANES, w), F32)],
        compiler_params=_params(3),
        name="rglru",
    )(proj, proj, conv_w, conv_b, gate_a_w, gate_a_b, gate_x_w, gate_x_b, lam)


def _qkv_kernel(x_ref, w_ref, c_ref, s1_ref, s2_ref, o_ref, acc_ref, xb_ref, *, tn, nj):
    @pl.when(pl.program_id(0) == 0)
    def _():
        acc_ref[...] = jnp.zeros_like(acc_ref)

    @pl.when(pl.program_id(0) % nj == 0)
    def _():
        xb_ref[...] = x_ref[...].astype(BF16)

    half = ROPE_DIM // 2
    rep = tn // ATTN_HEAD_DIM
    prev = acc_ref[...]
    c = jnp.concatenate([c_ref[0]] * rep, axis=1)
    s1 = jnp.concatenate([s1_ref[0]] * rep, axis=1)
    s2 = jnp.concatenate([s2_ref[0]] * rep, axis=1)
    roped = prev * c + pltpu.roll(prev, half, axis=1) * s1 + pltpu.roll(prev, tn - half, axis=1) * s2
    o_ref[...] = roped.astype(o_ref.dtype)
    acc_ref[...] = _dot(xb_ref[...], w_ref[...])


def _rope_tables(seqlen, scale):
    half = ROPE_DIM // 2
    pos = jnp.arange(seqlen, dtype=F32)
    inv_freq = ROPE_THETA ** (-jnp.arange(0, ROPE_DIM, 2, dtype=F32) / ROPE_DIM)
    ang = pos[:, None] * inv_freq[None, :]
    cos, sin = jnp.cos(ang), jnp.sin(ang)
    pad = ATTN_HEAD_DIM - ROPE_DIM
    c = jnp.concatenate([cos, cos, jnp.ones((seqlen, pad), F32)], axis=1)
    s1 = jnp.concatenate([jnp.zeros((seqlen, half), F32), sin, jnp.zeros((seqlen, pad), F32)], axis=1)
    s2 = jnp.concatenate([-sin, jnp.zeros((seqlen, half + pad), F32)], axis=1)
    ident = jnp.ones_like(c)
    zero = jnp.zeros_like(c)
    return (jnp.stack([c * scale, c, ident]), jnp.stack([s1 * scale, s1, zero]), jnp.stack([s2 * scale, s2, zero]))


def _qkv_rope(x, w_qkv, seqlen, tm=512, tn=2048):
    n, d = x.shape
    nout = w_qkv.shape[1]
    tm = min(tm, seqlen)
    tn = min(tn, d)
    c, s1, s2 = _rope_tables(seqlen, ATTN_HEAD_DIM**-0.5)
    nt = seqlen // tm
    ni, nj = n // tm, nout // tn
    last = ni * nj - 1
    cur = lambda s: jnp.minimum(s, last)
    lag = lambda s: jnp.maximum(s - 1, 0)
    tab_spec = pl.BlockSpec((1, tm, ATTN_HEAD_DIM), lambda s: ((lag(s) % nj) * tn // d, (lag(s) // nj) % nt, 0))
    return pl.pallas_call(
        functools.partial(_qkv_kernel, tn=tn, nj=nj),
        grid=(ni * nj + 1,),
        in_specs=[
            pl.BlockSpec((tm, d), lambda s: (cur(s) // nj, 0)),
            pl.BlockSpec((d, tn), lambda s: (0, cur(s) % nj)),
            tab_spec,
            tab_spec,
            tab_spec,
        ],
        out_specs=pl.BlockSpec((tm, tn), lambda s: (lag(s) // nj, lag(s) % nj)),
        out_shape=jax.ShapeDtypeStruct((n, nout), BF16),
        scratch_shapes=[pltpu.VMEM((tm, tn), F32), pltpu.VMEM((tm, d), BF16)],
        compiler_params=_params(1),
        name="qkv_rope",
    )(x, w_qkv, c, s1, s2)


def _moba_kernel(q_ref, k_ref, v_ref, o_ref, kmean_ref, vt_ref, sel_ref, s_ref, *, nblk, nh, nq):
    t = pl.program_id(2)
    blk, hd = MOBA_BLOCK, ATTN_HEAD_DIM

    heads = range(nh)
    tiles = range(nq)
    combos = [(a, h) for a in tiles for h in heads]
    lanes = [slice(h * hd, (h + 1) * hd) for h in heads]
    rows = [slice(a * blk, (a + 1) * blk) for a in tiles]
    own = [nq * t + a for a in tiles]

    @pl.when(t == 0)
    def _():
        for h in heads:
            kf = k_ref[:, lanes[h]].astype(F32).reshape(nblk, blk, hd)
            kmean_ref[h] = jnp.sum(kf, axis=1) * (1.0 / blk)
            vt_ref[h] = v_ref[:, lanes[h]].astype(F32).T.astype(BF16)

    q = {(a, h): q_ref[rows[a], lanes[h]] for a, h in combos}

    def scores(jp, a, h):
        off = pl.multiple_of(jp * (2 * blk), 2 * blk)
        return _dot_nt(k_ref[pl.ds(off, 2 * blk), lanes[h]], q[a, h])

    for a, h in combos:
        s_ref[a, 0, h] = scores(0, a, h)

    for a, h in combos:
        gate = _dot_nt(kmean_ref[h], q[a, h].astype(F32), precision=HIGHEST)
        blk_id = lax.broadcasted_iota(jnp.int32, gate.shape, 0).astype(F32)
        past = blk_id < own[a].astype(F32)
        g = jnp.where(past, gate, NEG_INF)
        sel = jnp.zeros_like(gate)
        for _ in range(MOBA_TOPK):
            top = jnp.max(g, axis=0, keepdims=True)
            first = jnp.min(jnp.where(g == top, blk_id, float(nblk)), axis=0, keepdims=True)
            hit = blk_id == first
            sel = jnp.where(hit, 1.0, sel)
            g = jnp.where(hit, MOBA_TAKEN, g)
        sel_ref[a, h] = jnp.where(past, sel, 0.0)

    key_pos = lax.broadcasted_iota(jnp.int32, (blk, blk), 0)
    qry_pos = lax.broadcasted_iota(jnp.int32, (blk, blk), 1)
    carry = {}
    for a, h in combos:
        off_own = pl.multiple_of(own[a] * blk, blk)
        s = _dot_nt(k_ref[pl.ds(off_own, blk), lanes[h]], q[a, h])
        s = jnp.where(key_pos <= qry_pos, s, NEG_INF)
        m = jnp.max(s, axis=0, keepdims=True)
        p = jnp.exp(s - m)
        l = jnp.sum(p, axis=0, keepdims=True)
        acc = _dot(vt_ref[h, :, pl.ds(off_own, blk)], p.astype(BF16))
        carry[a, h] = (m, l, acc)

    def make_body(a):
        def body(jp, carry):
            off = pl.multiple_of(jp * (2 * blk), 2 * blk)
            s_cur = [s_ref[a, jp % 2, h] for h in heads]
            jn = jnp.minimum(jp + 1, nblk // 2 - 1)
            for h in heads:
                s_ref[a, (jp + 1) % 2, h] = scores(jn, a, h)
            out = []
            for h in heads:
                m, l, acc = carry[h]
                s0 = jnp.where(sel_ref[a, h, pl.ds(2 * jp, 1), :] > 0.5, s_cur[h][:blk], NEG_INF)
                s1 = jnp.where(sel_ref[a, h, pl.ds(2 * jp + 1, 1), :] > 0.5, s_cur[h][blk:], NEG_INF)
                m_blk = jnp.maximum(jnp.max(s0, axis=0, keepdims=True), jnp.max(s1, axis=0, keepdims=True))
                m_new = jnp.maximum(m, m_blk)
                alpha = jnp.exp(m - m_new)
                p0 = jnp.exp(s0 - m_new)
                p1 = jnp.exp(s1 - m_new)
                l = alpha * l + jnp.sum(p0, axis=0, keepdims=True) + jnp.sum(p1, axis=0, keepdims=True)
                p = jnp.concatenate([p0.astype(BF16), p1.astype(BF16)], axis=0)
                acc = acc * alpha + _dot(vt_ref[h, :, pl.ds(off, 2 * blk)], p)
                out.append((m_new, l, acc))
            return tuple(out)

        return body

    done = {}
    for a in tiles:
        res = lax.fori_loop(0, (own[a] + 1) // 2, make_body(a), tuple(carry[a, h] for h in heads))
        for h in heads:
            done[a, h] = res[h]
    for a, h in combos:
        m, l, acc = done[a, h]
        o_ref[rows[a], lanes[h]] = (acc / l).T.astype(o_ref.dtype)


def _moba_attention(qkv, bsz, seqlen, nh=2, nq=8):
    n, d3 = qkv.shape
    d = d3 // 3
    ngrp = d // (nh * ATTN_HEAD_DIM)
    nblk = seqlen // MOBA_BLOCK
    assert nblk % 2 == 0 and nblk % nq == 0
    blk, hd = MOBA_BLOCK, ATTN_HEAD_DIM
    wl = nh * hd
    nt = nblk // nq
    return pl.pallas_call(
        functools.partial(_moba_kernel, nblk=nblk, nh=nh, nq=nq),
        grid=(bsz, ngrp, nt),
        in_specs=[
            pl.BlockSpec((nq * blk, wl), lambda b, h, t: (b * nt + t, h)),
            pl.BlockSpec((seqlen, wl), lambda b, h, t: (b, ngrp + h)),
            pl.BlockSpec((seqlen, wl), lambda b, h, t: (b, 2 * ngrp + h)),
        ],
        out_specs=pl.BlockSpec((nq * blk, wl), lambda b, h, t: (b * nt + t, h)),
        out_shape=jax.ShapeDtypeStruct((n, d), BF16),
        scratch_shapes=[pltpu.VMEM((nh, nblk, hd), F32), pltpu.VMEM((nh, hd, seqlen), BF16),
                        pltpu.VMEM((nq, nh, nblk, blk), F32), pltpu.VMEM((nq, 2, nh, 2 * blk, blk), F32)],
        compiler_params=_params(3),
        name="moba_attention",
    )(qkv, qkv, qkv)


def _token_shift_delta(x, prev_ref, is_start):
    prev_row = jnp.where(is_start, 0.0, prev_ref[SUBLANES - 1 : SUBLANES, :])
    row = lax.broadcasted_iota(jnp.int32, x.shape, 0)
    return jnp.where(row == 0, prev_row, pltpu.roll(x, 1, axis=0)) - x


def _rwkv_proj_kernel(x_ref, prev_ref, mu_ref, w_ref, o_ref, *, tm, seqlen):
    x = x_ref[...]
    xx = _token_shift_delta(x, prev_ref, (pl.program_id(1) * tm) % seqlen == 0)
    o_ref[0] = _dot((x + xx * mu_ref[0]).astype(BF16), w_ref[0])


def _rwkv_proj(x, mu, w_rkv, seqlen, tm=512):
    n, d = x.shape
    tm = min(tm, seqlen)
    step = tm // SUBLANES
    return pl.pallas_call(
        functools.partial(_rwkv_proj_kernel, tm=tm, seqlen=seqlen),
        grid=(3, n // tm),
        in_specs=[
            pl.BlockSpec((tm, d), lambda g, i: (i, 0)),
            pl.BlockSpec((SUBLANES, d), lambda g, i: (jnp.maximum(i * step - 1, 0), 0)),
            pl.BlockSpec((1, 1, d), lambda g, i: (g, 0, 0)),
            pl.BlockSpec((1, d, d), lambda g, i: (g, 0, 0)),
        ],
        out_specs=pl.BlockSpec((1, tm, d), lambda g, i: (g, i, 0)),
        out_shape=jax.ShapeDtypeStruct((3, n, d), F32),
        compiler_params=_params(2),
        name="rwkv_proj",
    )(x, x, mu.reshape(mu.shape[0], 1, d), w_rkv)


def _rwkv_lora_kernel(x_ref, prev_ref, mu_ref, ww1_ref, ww2_ref, aw1_ref, aw2_ref, gw1_ref, gw2_ref, wo_ref, ao_ref, go_ref, *, tm, seqlen):
    x = x_ref[...]
    xx = _token_shift_delta(x, prev_ref, (pl.program_id(0) * tm) % seqlen == 0)
    xw = (x + xx * mu_ref[3:4, :]).astype(BF16)
    xa = (x + xx * mu_ref[4:5, :]).astype(BF16)
    xg = (x + xx * mu_ref[5:6, :]).astype(BF16)
    wo_ref[...] = _dot(jnp.tanh(_dot(xw, ww1_ref[...])).astype(BF16), ww2_ref[...])
    ao_ref[...] = _dot(_dot(xa, aw1_ref[...]).astype(BF16), aw2_ref[...])
    go_ref[...] = _dot(jax.nn.sigmoid(_dot(xg, gw1_ref[...])).astype(BF16), gw2_ref[...])


def _pad_lora(w1, w2):
    rank = w1.shape[1]
    pad = (-rank) % LANES
    return jnp.pad(w1, ((0, 0), (0, pad))), jnp.pad(w2, ((0, pad), (0, 0)))


def _rwkv_lora(x, mu, w_w1, w_w2, a_w1, a_w2, g_w1, g_w2, seqlen, tm=512):
    n, d = x.shape
    tm = min(tm, seqlen)
    step = tm // SUBLANES
    ws = [*_pad_lora(w_w1, w_w2), *_pad_lora(a_w1, a_w2), *_pad_lora(g_w1, g_w2)]
    full = lambda a: pl.BlockSpec(a.shape, lambda i: (0, 0))
    out_spec = pl.BlockSpec((tm, d), lambda i: (i, 0))
    return pl.pallas_call(
        functools.partial(_rwkv_lora_kernel, tm=tm, seqlen=seqlen),
        grid=(n // tm,),
        in_specs=[
            pl.BlockSpec((tm, d), lambda i: (i, 0)),
            pl.BlockSpec((SUBLANES, d), lambda i: (jnp.maximum(i * step - 1, 0), 0)),
            full(mu),
            *[full(w) for w in ws],
        ],
        out_specs=[out_spec, out_spec, out_spec],
        out_shape=[jax.ShapeDtypeStruct((n, d), F32)] * 3,
        compiler_params=_params(1),
        name="rwkv_lora",
    )(x, x, mu, *ws)


def _rwkv_core_kernel(r_ref, k_ref, v_ref, wpre_ref, apre_ref, g_ref, w0_ref, a0_ref, kk_ref, ka_ref, rk_ref, lg_ref, lb_ref,
                      o_ref, st_ref, *, tb, npp):
    c = RWKV_CHUNK
    hd = RWKV_HEAD_DIM

    @pl.when(pl.program_id(2) == 0)
    def _():
        st_ref[...] = jnp.zeros_like(st_ref)

    lane = lax.broadcasted_iota(jnp.int32, (1, RWKV_PAIR), 1)
    m0 = jnp.where(lane < hd, 1.0, 0.0)
    m1 = 1.0 - m0
    ri = lax.broadcasted_iota(jnp.int32, (RWKV_PAIR, RWKV_PAIR), 0)
    ci = lax.broadcasted_iota(jnp.int32, (RWKV_PAIR, RWKV_PAIR), 1)
    same = (ri < hd) == (ci < hd)
    ones_bd = jnp.where(same, 1.0, 0.0)
    strict = jnp.where(same & (ci < ri), 1.0, 0.0)
    incl = jnp.where(same & (ci <= ri), 1.0, 0.0)
    eye = jnp.where(ri == ci, 1.0, 0.0)

    def head_sum(z):
        hi, lo = _split_bf16(z)
        return _dot(jnp.concatenate([hi, lo], axis=1), ones2_b16)

    def stack(z):
        return jnp.concatenate([z * m0, z * m1], axis=0)

    def b16(z):
        return z.astype(BF16)

    ones2_b16 = b16(jnp.concatenate([ones_bd, ones_bd], axis=0))
    zeros_sq = jnp.zeros((RWKV_PAIR, RWKV_PAIR), F32)

    n2 = RWKV_PAIR
    nch = tb // c
    rowc = lax.broadcasted_iota(jnp.int32, (tb, n2), 0) % c

    prep = []
    for p in range(npp):
        ln = slice(p * n2, (p + 1) * n2)
        r = r_ref[0, :, ln]
        k = k_ref[0, :, ln]
        v = v_ref[0, :, ln]
        z = w0_ref[:, ln] + wpre_ref[:, ln]
        w_log = jnp.minimum(z, 0.0) - jnp.log(1.0 + jnp.exp(-jnp.abs(z))) - 0.5
        lw = -jnp.exp(w_log)
        a = jax.nn.sigmoid(a0_ref[:, ln] + apre_ref[:, ln])
        kk = k * kk_ref[:, ln]
        kk = kk / jnp.maximum(jnp.sqrt(head_sum(kk * kk)), 1e-12)
        k2 = k * (1.0 + (a - 1.0) * ka_ref[:, ln])
        bv = kk * a
        cl = lw
        d = 1
        while d < c:
            cl = cl + jnp.where(rowc >= d, pltpu.roll(cl, d, axis=0), 0.0)
            d *= 2
        prep.append((r, k2, v, kk, bv, lw, cl))

    items = [(p, ch) for p in range(npp) for ch in range(nch)]
    rt, at16, v16, lhs16, diag, a_ab, a_ak16 = {}, {}, {}, {}, {}, {}, {}
    for it in items:
        p, ch = it
        r, k2, v, kk, bv, lw, cl = prep[p]
        sl = slice(ch * c, (ch + 1) * c)
        clc = cl[sl]
        cle = clc[c - 1 : c, :]
        e_neg = jnp.exp(-clc)
        e_end = jnp.exp(cle - clc)
        rt_s = stack(r[sl] * jnp.exp(clc))
        at_s = stack(-kk[sl] * jnp.exp(clc - lw[sl]))
        gmat = _dot_nt(b16(jnp.concatenate([at_s, rt_s], axis=0)),
                       b16(jnp.concatenate([stack(bv[sl] * e_neg), stack(k2[sl] * e_neg)], axis=0)))
        a_ab[it] = gmat[:n2, :n2] * strict
        a_ak16[it] = b16(gmat[:n2, n2:] * strict)
        ends_t = jnp.concatenate([stack(bv[sl] * e_end).T, stack(k2[sl] * e_end).T], axis=1)
        lhs16[it] = b16(jnp.concatenate([gmat[n2:] * jnp.concatenate([incl, incl], axis=1), ends_t], axis=0))
        rt[it] = rt_s
        at16[it] = b16(at_s)
        v16[it] = b16(stack(v[sl]))
        diag[it] = eye * jnp.exp(cle)

    av16 = {it: b16(_dot(a_ak16[it], v16[it])) for it in items}
    tinv = {it: eye + a_ab[it] for it in items}
    l16 = {it: b16(a_ab[it]) for it in items}
    pw = {it: _dot(l16[it], l16[it]) for it in items}
    span = 2
    while span < c:
        span *= 2
        for it in items:
            p16 = b16(pw[it])
            if span < c:
                res = _dot(p16, jnp.concatenate([b16(tinv[it]), p16], axis=1))
                tinv[it] = tinv[it] + res[:, :n2]
                pw[it] = res[:, n2:]
            else:
                tinv[it] = tinv[it] + _dot(p16, b16(tinv[it]))

    big = {}
    for it in items:
        wu16 = b16(_dot(b16(tinv[it]), jnp.concatenate([at16[it], av16[it]], axis=1)))
        rhs16 = jnp.concatenate([wu16, jnp.concatenate([b16(zeros_sq), v16[it]], axis=1)], axis=0)
        big[it] = _dot(lhs16[it], rhs16)

    st = {p: st_ref[p] for p in range(npp)}
    st_in = {}
    for ch in range(nch):
        for p in range(npp):
            it = (p, ch)
            st_in[it] = st[p]
            mc16 = b16(diag[it] + big[it][n2:, :n2])
            st_hi, st_lo = _split_bf16(st[p])
            st[p] = _dot(jnp.concatenate([mc16, mc16], axis=1), jnp.concatenate([st_hi, st_lo], axis=0)) + big[it][n2:, n2:]
    for p in range(npp):
        st_ref[p] = st[p]

    for p in range(npp):
        ln = slice(p * n2, (p + 1) * n2)
        r, k2, v, kk, bv, lw, cl = prep[p]
        ys = []
        for ch in range(nch):
            it = (p, ch)
            y_s = _dot(b16(rt[it] + big[it][:n2, :n2]), b16(st_in[it])) + big[it][:n2, n2:]
            ys.append(y_s[:c] + y_s[c:])
        y = jnp.concatenate(ys, axis=0)
        mean = head_sum(y) * (1.0 / hd)
        yc = y - mean
        var = head_sum(yc * yc) * (1.0 / hd)
        yn = yc * lax.rsqrt(var + RWKV_LNX_EPS) * lg_ref[:, ln] + lb_ref[:, ln]
        bonus = head_sum(r * k2 * rk_ref[:, ln]) * v
        o_ref[:, ln] = ((yn + bonus) * g_ref[:, ln]).astype(o_ref.dtype)


def _rwkv_core(rkv, w_pre, a_pre, g, w0, a0, k_k, k_a, r_k, lnx_g, lnx_b, bsz, seqlen, tb=512, npp=4):
    _, n, d = rkv.shape
    tb = min(tb, seqlen)
    nt = seqlen // tb
    wl = npp * RWKV_PAIR
    rkv_spec = lambda j: pl.BlockSpec((1, tb, wl), lambda b, p, t: (j, b * nt + t, p))
    act_spec = pl.BlockSpec((tb, wl), lambda b, p, t: (b * nt + t, p))
    row_spec = pl.BlockSpec((1, wl), lambda b, p, t: (0, p))
    return pl.pallas_call(
        functools.partial(_rwkv_core_kernel, tb=tb, npp=npp),
        grid=(bsz, d // wl, nt),
        in_specs=[rkv_spec(0), rkv_spec(1), rkv_spec(2), act_spec, act_spec, act_spec] + [row_spec] * 7,
        out_specs=act_spec,
        out_shape=jax.ShapeDtypeStruct((n, d), BF16),
        scratch_shapes=[pltpu.VMEM((npp, RWKV_PAIR, RWKV_PAIR), F32)],
        compiler_params=_params(3),
        name="rwkv_core",
    )(rkv, rkv, rkv, w_pre, a_pre, g, w0, a0, k_k, k_a, r_k, lnx_g, lnx_b)


def _pool_kernel(x_ref, prev_ref, w_ref, sc_ref, g_ref, b_ref, o_ref, *, tm, seqlen, alpha):
    start = (pl.program_id(0) * tm) % seqlen
    x = x_ref[...]
    prev = jnp.where(start == 0, 0.0, prev_ref[...])
    xe = jnp.concatenate([prev, x], axis=0)
    pos = start + lax.broadcasted_iota(jnp.int32, (tm, 1), 0)
    gw = w_ref.shape[1]
    ys = []
    for gi, win in enumerate(POOL_WINDOWS):
        sl = slice(gi * gw, (gi + 1) * gw)
        s = xe[:, sl]
        d = 1
        while d < win:
            s = s + pltpu.roll(s, d, axis=0)
            d *= 2
        cnt = jnp.minimum(pos + 1, win).astype(F32)
        pooled = s[POOL_HALO:] / cnt - x[:, sl]
        ys.append(_dot(pooled.astype(BF16), w_ref[gi]))
    y = jnp.concatenate(ys, axis=1) * sc_ref[...]
    o_ref[...] = _layer_norm_rows(alpha * x + y, g_ref[...], b_ref[...])


def _pool_ln(x, w_pool, scale, g, b, seqlen, alpha, tm=512):
    n, d = x.shape
    tm = min(tm, seqlen)
    step = tm // POOL_HALO
    row_spec = pl.BlockSpec((1, d), lambda i: (0, 0))
    return pl.pallas_call(
        functools.partial(_pool_kernel, tm=tm, seqlen=seqlen, alpha=alpha),
        grid=(n // tm,),
        in_specs=[
            pl.BlockSpec((tm, d), lambda i: (i, 0)),
            pl.BlockSpec((POOL_HALO, d), lambda i: (jnp.maximum(i * step - 1, 0), 0)),
            pl.BlockSpec(w_pool.shape, lambda i: (0, 0, 0)),
            row_spec,
            row_spec,
            row_spec,
        ],
        out_specs=pl.BlockSpec((tm, d), lambda i: (i, 0)),
        out_shape=jax.ShapeDtypeStruct((n, d), F32),
        compiler_params=_params(1),
        name="pool_ln",
    )(x, x, w_pool, scale, g, b)


def kernel(x, ln_g, ln_b, mlp_w1, mlp_w2, rg_w_in, rg_conv_w, rg_conv_b, rg_gate_a_w, rg_gate_a_b, rg_gate_x_w, rg_gate_x_b, rg_lambda, rg_w_out, moba_w_qkv, moba_w_out, rwkv_mu, rwkv_w_rkv, rwkv_w0, rwkv_w_w1, rwkv_w_w2, rwkv_a0, rwkv_a_w1, rwkv_a_w2, rwkv_g_w1, rwkv_g_w2, rwkv_k_k, rwkv_k_a, rwkv_r_k, rwkv_lnx_g, rwkv_lnx_b, rwkv_w_out, pool_w, pool_scale):
    bsz, seqlen, d = x.shape
    depth = ln_g.shape[0]
    n_mixers = 4
    alpha = (2.0 * depth) ** 0.25
    bf = lambda w: w.astype(BF16)
    row = lambda p: p.reshape(1, -1)
    mlp_w1_b16, mlp_w2_b16 = bf(mlp_w1), bf(mlp_w2)

    h = x.reshape(bsz * seqlen, d)
    counts = [0] * n_mixers
    for layer in range(depth):
        m = layer % n_mixers
        j = counts[m]
        counts[m] += 1
        g0, b0 = row(ln_g[layer, 0]), row(ln_b[layer, 0])
        if m == 0:
            proj = _matmul(h, bf(rg_w_in[j]), F32)
            mixed = _rglru(proj, rg_conv_w[j], row(rg_conv_b[j]), bf(rg_gate_a_w[j]), row(rg_gate_a_b[j]),
                           bf(rg_gate_x_w[j]), row(rg_gate_x_b[j]), row(rg_lambda[j]), bsz, seqlen)
            h = _matmul_ln(mixed, bf(rg_w_out[j]), h, g0, b0, alpha)
        elif m == 1:
            qkv = _qkv_rope(h, bf(moba_w_qkv[j]), seqlen)
            att = _moba_attention(qkv, bsz, seqlen)
            h = _matmul_ln(att, bf(moba_w_out[j]), h, g0, b0, alpha)
        elif m == 2:
            rkv = _rwkv_proj(h, rwkv_mu[j], bf(rwkv_w_rkv[j]), seqlen)
            w_pre, a_pre, gate = _rwkv_lora(h, rwkv_mu[j], bf(rwkv_w_w1[j]), bf(rwkv_w_w2[j]), bf(rwkv_a_w1[j]),
                                            bf(rwkv_a_w2[j]), bf(rwkv_g_w1[j]), bf(rwkv_g_w2[j]), seqlen)
            mixed = _rwkv_core(rkv, w_pre, a_pre, gate, row(rwkv_w0[j]), row(rwkv_a0[j]), row(rwkv_k_k[j]), row(rwkv_k_a[j]),
                               row(rwkv_r_k[j]), row(rwkv_lnx_g[j]), row(rwkv_lnx_b[j]), bsz, seqlen)
            h = _matmul_ln(mixed, bf(rwkv_w_out[j]), h, g0, b0, alpha)
        else:
            h = _pool_ln(h, bf(pool_w[j]), row(pool_scale[j]), g0, b0, seqlen, alpha)
        h = _mlp_ln(h, mlp_w1_b16, mlp_w2_b16, layer, row(ln_g[layer, 1]), row(ln_b[layer, 1]), alpha)
    return h.reshape(bsz, seqlen, d)
```
